```python
import jax, jax.numpy as jnp
from jax import lax
import numpy as np

D_MODEL = 1024
BATCH = 4
SEQ = 8192
DEPTH = 4
DEC_BATCH = 8
DEC_SEQ = 64
PAST_LEN = 1024

CHUNK = 64
Q_BLOCK = 128
EPS = 1e-6
H_MLA = 8
NOPE_DIM = 64
ROPE_DIM = 32
V_DIM = 64
Q_LORA = 384
KV_LORA = 256
ROPE_THETA = 10000.0
MLA_SCALE = (NOPE_DIM + ROPE_DIM) ** -0.5
H_FOX = 8
FOX_DIM = 64
FOX_SCALE = FOX_DIM ** -0.5
MIX_MLA = H_MLA * V_DIM
MIX_FOX = H_FOX * FOX_DIM
MIX_WIDTH = MIX_MLA + MIX_FOX
IN_SIZES = (Q_LORA, KV_LORA, ROPE_DIM, MIX_FOX, MIX_FOX, MIX_FOX, H_FOX)
N_IN = Q_LORA + KV_LORA + ROPE_DIM + 3 * MIX_FOX + H_FOX
N_GROUPS = 4
EXPERTS_PER_GROUP = 8
N_EXPERTS = N_GROUPS * EXPERTS_PER_GROUP
TOP_K_EXPERTS = 2
D_EXPERT = 256

kernel_name = 'hybrid_mla_fox_hmoe_adaln_stream_step'


def _rms_norm(x, g):
    xf = x.astype(jnp.float32)
    y = xf * lax.rsqrt(jnp.mean(xf * xf, axis=-1, keepdims=True) + EPS)
    return (y * g.astype(jnp.float32)).astype(x.dtype)


def _rope(x, pos):
    half = ROPE_DIM // 2
    inv = ROPE_THETA ** (-jnp.arange(half, dtype=jnp.float32) / half)
    ang = pos.astype(jnp.float32)[:, None] * inv[None, :]
    shape = (pos.shape[0],) + (1,) * (x.ndim - 3) + (half,)
    cos, sin = jnp.cos(ang).reshape(shape), jnp.sin(ang).reshape(shape)
    xf = x.astype(jnp.float32)
    x1, x2 = xf[..., :half], xf[..., half:]
    return jnp.concatenate([x1 * cos - x2 * sin, x2 * cos + x1 * sin], axis=-1).astype(x.dtype)


def _attend(q, k, v, q_pos, k_pos, scale, chunk_causal, fq=None, fk=None):
    s = jnp.einsum('bqhd,bkhd->bhqk', q, k, preferred_element_type=jnp.float32) * scale
    if fq is not None:
        s = s + (jnp.swapaxes(fq, 1, 2)[:, :, :, None] - jnp.swapaxes(fk, 1, 2)[:, :, None, :])
    if chunk_causal:
        allowed = (k_pos[None, :] // CHUNK) <= (q_pos[:, None] // CHUNK)
    else:
        allowed = k_pos[None, :] <= q_pos[:, None]
    s = jnp.where(allowed[None, None], s, -jnp.inf)
    p = jax.nn.softmax(s, axis=-1)
    return jnp.einsum('bhqk,bkhd->bqhd', p.astype(v.dtype), v)


def _attend_blocked(q, k, v, scale, chunk_causal, fq=None, fk=None):
    B, S, H, dq = q.shape
    nb = S // Q_BLOCK
    k_pos = jnp.arange(S, dtype=jnp.int32)
    q_blocks = jnp.swapaxes(q.reshape(B, nb, Q_BLOCK, H, dq), 0, 1)
    pos_blocks = k_pos.reshape(nb, Q_BLOCK)
    if fq is None:
        out = lax.map(lambda a: _attend(a[0], k, v, a[1], k_pos, scale, chunk_causal),
                      (q_blocks, pos_blocks))
    else:
        f_blocks = jnp.swapaxes(fq.reshape(B, nb, Q_BLOCK, H), 0, 1)
        out = lax.map(lambda a: _attend(a[0], k, v, a[1], k_pos, scale, chunk_causal, a[2], fk),
                      (q_blocks, pos_blocks, f_blocks))
    return jnp.swapaxes(out, 0, 1).reshape(B, S, H, v.shape[-1])


def _token_mixer(h, past, w_in, b_forget, q_norm, w_q_up, kv_norm, w_kv_up,
                 out_norm_mla, out_norm_fox, w_out):
    B, T, _ = h.shape
    p0 = 0 if past is None else past[0].shape[1]
    q_pos = p0 + jnp.arange(T, dtype=jnp.int32)
    proj = jnp.einsum('btd,dn->btn', h, w_in)
    points = [int(p) for p in np.cumsum(IN_SIZES)[:-1]]
    q_lat, kv_lat, kr_raw, q_f, k_f, v_f, f_logit = jnp.split(proj, points, axis=-1)
    q = jnp.einsum('btr,rn->btn', _rms_norm(q_lat, q_norm), w_q_up).reshape(B, T, H_MLA, NOPE_DIM + ROPE_DIM)
    q = jnp.concatenate([q[..., :NOPE_DIM], _rope(q[..., NOPE_DIM:], q_pos)], axis=-1)
    c_kv = _rms_norm(kv_lat, kv_norm)
    k_rope = _rope(kr_raw, q_pos)
    q_f = q_f.reshape(B, T, H_FOX, FOX_DIM)
    k_f = k_f.reshape(B, T, H_FOX, FOX_DIM)
    v_f = v_f.reshape(B, T, H_FOX, FOX_DIM)
    log_f = jax.nn.log_sigmoid(f_logit.astype(jnp.float32) + b_forget.astype(jnp.float32))
    new_rows = (c_kv, k_rope, k_f, v_f, log_f)
    if past is None:
        rows = new_rows
    else:
        rows = tuple(jnp.concatenate([p.astype(n.dtype), n], axis=1) for p, n in zip(past, new_rows))
    ckv_all, kr_all, kf_all, vf_all, logf_all = rows
    S = ckv_all.shape[1]
    kv = jnp.einsum('bsr,rn->bsn', ckv_all, w_kv_up).reshape(B, S, H_MLA, NOPE_DIM + V_DIM)
    k_m = jnp.concatenate([kv[..., :NOPE_DIM],
                           jnp.broadcast_to(kr_all[:, :, None, :], (B, S, H_MLA, ROPE_DIM))], axis=-1)
    v_m = kv[..., NOPE_DIM:]
    f_cum = jnp.cumsum(logf_all.astype(jnp.float32), axis=1)
    f_q = f_cum[:, p0:]
    if past is None:
        o_m = _attend_blocked(q, k_m, v_m, MLA_SCALE, True)
        o_f = _attend_blocked(q_f, kf_all, vf_all, FOX_SCALE, False, f_q, f_cum)
    else:
        k_pos = jnp.arange(S, dtype=jnp.int32)
        o_m = _attend(q, k_m, v_m, q_pos, k_pos, MLA_SCALE, True)
        o_f = _attend(q_f, kf_all, vf_all, q_pos, k_pos, FOX_SCALE, False, f_q, f_cum)
    o = jnp.concatenate([_rms_norm(o_m.reshape(B, T, MIX_MLA), out_norm_mla),
                         _rms_norm(o_f.reshape(B, T, MIX_FOX), out_norm_fox)], axis=-1)
    return jnp.einsum('btm,md->btd', o, w_out), new_rows


def _hier_moe(h, w_router_group, b_router_group, w_router_expert, b_router_expert, w_gate, w_up, w_down):
    B, T, D = h.shape
    xt = h.reshape(B * T, D)
    g_prob = jax.nn.softmax(jnp.einsum('nd,dg->ng', xt, w_router_group).astype(jnp.float32)
                            + b_router_group.astype(jnp.float32), axis=-1)
    g_idx = jnp.argmax(g_prob, axis=-1)
    g_p = jnp.max(g_prob, axis=-1)
    g_onehot = jax.nn.one_hot(g_idx, N_GROUPS, dtype=jnp.float32)
    e_logits = (jnp.einsum('nd,de->ne', xt, w_router_expert).astype(jnp.float32)
                + b_router_expert.astype(jnp.float32)).reshape(-1, N_GROUPS, EXPERTS_PER_GROUP)
    e_in = jnp.sum(e_logits * g_onehot[:, :, None], axis=1)
    e_p, e_idx = lax.top_k(jax.nn.softmax(e_in, axis=-1), TOP_K_EXPERTS)
    e_w = e_p / jnp.sum(e_p, axis=-1, keepdims=True)
    gate_in = jnp.sum(jax.nn.one_hot(e_idx, EXPERTS_PER_GROUP, dtype=jnp.float32) * e_w[..., None], axis=1)
    gate = g_onehot[:, :, None] * (g_p[:, None, None] * gate_in[:, None, :])
    y = jnp.zeros_like(xt)
    for g in range(N_GROUPS):
        sl = slice(g * EXPERTS_PER_GROUP, (g + 1) * EXPERTS_PER_GROUP)
        a = jnp.einsum('nd,edf->nef', xt, w_gate[sl])
        u = jnp.einsum('nd,edf->nef', xt, w_up[sl])
        act = jax.nn.silu(a) * u * gate[:, g, :, None].astype(xt.dtype)
        y = y + jnp.einsum('nef,efd->nd', act, w_down[sl])
    return y.reshape(B, T, D)


def _layer(x, c, past, w_mod, b_mod, norm_attn, w_in, b_forget, q_norm, w_q_up,
           kv_norm, w_kv_up, out_norm_mla, out_norm_fox, w_out, norm_ffn,
           w_router_group, b_router_group, w_router_expert, b_router_expert,
           w_gate, w_up, w_down):
    mod = (jnp.einsum('bd,dn->bn', c, w_mod) + b_mod)[:, None, :]
    shift1, scale1, gate1, shift2, scale2, gate2 = jnp.split(mod, 6, axis=-1)
    h = _rms_norm(x, norm_attn) * (1 + scale1) + shift1
    mix, new_rows = _token_mixer(h, past, w_in, b_forget, q_norm, w_q_up, kv_norm, w_kv_up,
                                 out_norm_mla, out_norm_fox, w_out)
    x = x + gate1 * mix
    h = _rms_norm(x, norm_ffn) * (1 + scale2) + shift2
    x = x + gate2 * _hier_moe(h, w_router_group, b_router_group, w_router_expert, b_router_expert,
                              w_gate, w_up, w_down)
    return x, new_rows


def setup_inputs(seed: int = 0) -> dict:
    key = jax.random.key(seed)
    ks = jax.random.split(key, 32)
    nrm = jax.random.normal
    f32 = jnp.float32
    D = D_MODEL
    return {
        'x_prompt': nrm(ks[0], (BATCH, SEQ, D), f32),
        'x_sample': nrm(ks[1], (DEC_BATCH, DEC_SEQ, D), f32),
        'cache_mla_ckv': nrm(ks[2], (DEPTH, DEC_BATCH, PAST_LEN, KV_LORA), f32),
        'cache_mla_krope': nrm(ks[3], (DEPTH, DEC_BATCH, PAST_LEN, ROPE_DIM), f32),
        'cache_fox_k': nrm(ks[4], (DEPTH, DEC_BATCH, PAST_LEN, H_FOX, FOX_DIM), f32),
        'cache_fox_v': nrm(ks[5], (DEPTH, DEC_BATCH, PAST_LEN, H_FOX, FOX_DIM), f32),
        'cache_fox_logf': jax.nn.log_sigmoid(nrm(ks[6], (DEPTH, DEC_BATCH, PAST_LEN, H_FOX), f32) + 3.0),
        'c_prompt': nrm(ks[7], (BATCH, D), f32),
        'c_sample': nrm(ks[8], (DEC_BATCH, D), f32),
        'w_mod': 0.3 * D ** -0.5 * nrm(ks[9], (DEPTH, D, 6 * D), f32),
        'b_mod': 0.01 * nrm(ks[10], (DEPTH, 6 * D), f32),
        'norm_attn': 1.0 + 0.05 * nrm(ks[11], (DEPTH, D), f32),
        'w_in': D ** -0.5 * nrm(ks[12], (DEPTH, D, N_IN), f32),
        'b_forget': jnp.linspace(1.0, 6.0, H_FOX, dtype=f32)[None, :] + 0.1 * nrm(ks[13], (DEPTH, H_FOX), f32),
        'q_norm': 1.0 + 0.05 * nrm(ks[14], (DEPTH, Q_LORA), f32),
        'w_q_up': Q_LORA ** -0.5 * nrm(ks[15], (DEPTH, Q_LORA, H_MLA * (NOPE_DIM + ROPE_DIM)), f32),
        'kv_norm': 1.0 + 0.05 * nrm(ks[16], (DEPTH, KV_LORA), f32),
        'w_kv_up': KV_LORA ** -0.5 * nrm(ks[17], (DEPTH, KV_LORA, H_MLA * (NOPE_DIM + V_DIM)), f32),
        'out_norm_mla': 1.0 + 0.05 * nrm(ks[18], (DEPTH, MIX_MLA), f32),
        'out_norm_fox': 1.0 + 0.05 * nrm(ks[19], (DEPTH, MIX_FOX), f32),
        'w_out': MIX_WIDTH ** -0.5 * nrm(ks[20], (DEPTH, MIX_WIDTH, D), f32),
        'norm_ffn': 1.0 + 0.05 * nrm(ks[21], (DEPTH, D), f32),
        'w_router_group': D ** -0.5 * nrm(ks[22], (DEPTH, D, N_GROUPS), f32),
        'b_router_group': 0.01 * nrm(ks[23], (DEPTH, N_GROUPS), f32),
        'w_router_expert': D ** -0.5 * nrm(ks[24], (DEPTH, D, N_EXPERTS), f32),
        'b_router_expert': 0.01 * nrm(ks[25], (DEPTH, N_EXPERTS), f32),
        'w_gate': D ** -0.5 * nrm(ks[26], (DEPTH, N_EXPERTS, D, D_EXPERT), f32),
        'w_up': D ** -0.5 * nrm(ks[27], (DEPTH, N_EXPERTS, D, D_EXPERT), f32),
        'w_down': D_EXPERT ** -0.5 * nrm(ks[28], (DEPTH, N_EXPERTS, D_EXPERT, D), f32),
        'final_norm': 1.0 + 0.05 * nrm(ks[29], (D,), f32),
    }


def reference(x_prompt, x_sample, cache_mla_ckv, cache_mla_krope, cache_fox_k, cache_fox_v,
              cache_fox_logf, c_prompt, c_sample, w_mod, b_mod, norm_attn, w_in, b_forget,
              q_norm, w_q_up, kv_norm, w_kv_up, out_norm_mla, out_norm_fox, w_out, norm_ffn,
              w_router_group, b_router_group, w_router_expert, b_router_expert,
              w_gate, w_up, w_down, final_norm):
    y_p, y_s = x_prompt, x_sample
    rows_p, rows_s = [], []
    for l in range(DEPTH):
        lw = (w_mod[l], b_mod[l], norm_attn[l], w_in[l], b_forget[l], q_norm[l], w_q_up[l],
              kv_norm[l], w_kv_up[l], out_norm_mla[l], out_norm_fox[l], w_out[l], norm_ffn[l],
              w_router_group[l], b_router_group[l], w_router_expert[l], b_router_expert[l],
              w_gate[l], w_up[l], w_down[l])
        y_p, r_p = _layer(y_p, c_prompt, None, *lw)
        past = (cache_mla_ckv[l], cache_mla_krope[l], cache_fox_k[l], cache_fox_v[l], cache_fox_logf[l])
        y_s, r_s = _layer(y_s, c_sample, past, *lw)
        rows_p.append(r_p)
        rows_s.append(r_s)
    y_prompt = _rms_norm(y_p, final_norm)
    y_sample = _rms_norm(y_s, final_norm)
    new_mla_ckv_prompt = jnp.stack([r[0] for r in rows_p])
    new_mla_krope_prompt = jnp.stack([r[1] for r in rows_p])
    new_fox_k_prompt = jnp.stack([r[2] for r in rows_p])
    new_fox_v_prompt = jnp.stack([r[3] for r in rows_p])
    new_fox_logf_prompt = jnp.stack([r[4] for r in rows_p])
    new_mla_ckv_sample = jnp.stack([r[0] for r in rows_s])
    new_mla_krope_sample = jnp.stack([r[1] for r in rows_s])
    new_fox_k_sample = jnp.stack([r[2] for r in rows_s])
    new_fox_v_sample = jnp.stack([r[3] for r in rows_s])
    new_fox_logf_sample = jnp.stack([r[4] for r in rows_s])
    return (y_prompt, y_sample,
            new_mla_ckv_prompt, new_mla_krope_prompt, new_fox_k_prompt, new_fox_v_prompt, new_fox_logf_prompt,
            new_mla_ckv_sample, new_mla_krope_sample, new_fox_k_sample, new_fox_v_sample, new_fox_logf_sample)
```

```python
import functools

import numpy as np
import jax
import jax.numpy as jnp
from jax import lax
from jax.experimental import pallas as pl
from jax.experimental.pallas import tpu as pltpu

CHUNK = 64
CHUNK_SHIFT = 6
EPS = 1e-6
H_MLA = 8
NOPE_DIM = 64
ROPE_DIM = 32
V_DIM = 64
Q_LORA = 384
KV_LORA = 256
ROPE_THETA = 10000.0
MLA_SCALE = (NOPE_DIM + ROPE_DIM) ** -0.5
H_FOX = 8
FOX_DIM = 64
FOX_SCALE = FOX_DIM ** -0.5
N_GROUPS = 4
EXPERTS_PER_GROUP = 8
N_EXPERTS = N_GROUPS * EXPERTS_PER_GROUP
D_EXPERT = 256

LANES = 128
HEAD_PAD = LANES
ROUTER_OFF = N_GROUPS
NEG = -1e30
VMEM_LIMIT = 56 * 1024 * 1024

F32 = jnp.float32
BF16 = jnp.bfloat16


def _cparams(sem):
    return pltpu.CompilerParams(dimension_semantics=sem, vmem_limit_bytes=VMEM_LIMIT)


def _rms(x, g):
    return x * lax.rsqrt(jnp.mean(x * x, axis=-1, keepdims=True) + EPS) * g


def _dot(a, b):
    return jnp.dot(a, b, preferred_element_type=F32)


def _split3(x):
    hi = x.astype(BF16).astype(F32)
    r = x - hi
    mid = r.astype(BF16).astype(F32)
    lo = (r - mid).astype(BF16).astype(F32)
    return hi, mid, lo


def _mod_kernel(c_ref, w_ref, b_ref, o_ref):
    o_ref[0] = _dot(c_ref[...].astype(BF16), w_ref[0].astype(BF16)) + b_ref[0]


def _mod_all(c_all, w_mod, b_mod):
    L, D, N6 = w_mod.shape
    R = c_all.shape[0]
    TN = N6 // 4
    return pl.pallas_call(
        _mod_kernel,
        out_shape=jax.ShapeDtypeStruct((L, R, N6), F32),
        grid=(L, N6 // TN),
        in_specs=[pl.BlockSpec((R, D), lambda l, j: (0, 0)),
                  pl.BlockSpec((1, D, TN), lambda l, j: (l, 0, j)),
                  pl.BlockSpec((1, 1, TN), lambda l, j: (l, 0, j))],
        out_specs=pl.BlockSpec((1, R, TN), lambda l, j: (l, 0, j)),
        compiler_params=_cparams(("arbitrary", "arbitrary")),
        name="adaln_mod",
    )(c_all, w_mod, b_mod.reshape(L, 1, N6))


def _fox_cum_aug(logf, carry_row, tril3_ref, pq_ref, pk_ref):
    hi, mid, lo = _split3(logf)
    stacked = jnp.concatenate([hi.astype(BF16), mid.astype(BF16), lo.astype(BF16)], axis=0)
    cum = _dot(tril3_ref[...], stacked) + carry_row
    lane = lax.broadcasted_iota(jnp.int32, cum.shape, 1)
    cum = jnp.where(lane < H_FOX, cum, 0.0)
    fh, fm, fl = _split3(cum)
    a3 = (fh + pltpu.roll(fm, H_FOX, axis=1) + pltpu.roll(fl, 2 * H_FOX, axis=1)
          + jnp.where(lane == 3 * H_FOX, 1.0, 0.0)).astype(BF16)
    augq = _dot(a3, pq_ref[...]).astype(BF16)
    augk = _dot(a3, pk_ref[...]).astype(BF16)
    return cum, augq, augk


def _aug_placement():
    pq = np.zeros((LANES, H_FOX * FOX_DIM), np.float32)
    pk = np.zeros((LANES, H_FOX * FOX_DIM), np.float32)
    for h in range(H_FOX):
        base = h * FOX_DIM
        for j in range(3):
            pq[3 * H_FOX, base + j] = 1.0
            pq[j * H_FOX + h, base + 3 + j] = 1.0
            pk[j * H_FOX + h, base + j] = -1.0
            pk[3 * H_FOX, base + 3 + j] = 1.0
    return jnp.asarray(pq, BF16), jnp.asarray(pk, BF16)


def _tril3(tm):
    t = np.tril(np.ones((tm, tm), np.float32))
    return jnp.asarray(np.concatenate([t, t, t], axis=1), BF16)


def _rope_lanes(x, c, s1, s2):
    return x * c + pltpu.roll(x, LANES - ROPE_DIM // 2, axis=1) * s1 + pltpu.roll(x, ROPE_DIM // 2, axis=1) * s2


def _pre_kernel(x_ref, mod_ref, na_ref, wq_ref, wkv_ref, wkr_ref, wfq_ref, wfk_ref, wfv_ref, wfl_ref, bfl_ref,
                qn_ref, wqu_ref, kvn_ref, cq_ref, s1q_ref, s2q_ref, ck_ref, s1k_ref, s2k_ref,
                tril3_ref, pq_ref, pk_ref, cinit_ref,
                ckv_ref, kro_ref, krp_ref, kf_ref, vf_ref, lf_ref, qm_ref, qf_ref, kfb_ref, vfb_ref,
                aq_ref, ak_ref, carry_ref):
    @pl.when(pl.program_id(1) == 0)
    def _():
        carry_ref[...] = cinit_ref[0]

    x = x_ref[...]
    shift = mod_ref[0, 0:1, :]
    scale = mod_ref[0, 1:2, :]
    h = _rms(x, na_ref[...]) * (1.0 + scale) + shift
    hb = h.astype(BF16)

    qn = _rms(_dot(hb, wq_ref[...]), qn_ref[...]).astype(BF16)
    q = _dot(qn, wqu_ref[...])
    cq, s1q, s2q = cq_ref[...], s1q_ref[...], s2q_ref[...]
    for hh in range(H_MLA):
        sl = slice(hh * HEAD_PAD, (hh + 1) * HEAD_PAD)
        qm_ref[:, sl] = _rope_lanes(q[:, sl], cq, s1q, s2q).astype(BF16)

    ckv_ref[...] = _rms(_dot(hb, wkv_ref[...]), kvn_ref[...])
    kr = _rope_lanes(_dot(hb, wkr_ref[...]), ck_ref[...], s1k_ref[...], s2k_ref[...])
    kro_ref[...] = kr[:, :ROPE_DIM]
    krp_ref[...] = kr.astype(BF16)

    qf_ref[...] = (_dot(hb, wfq_ref[...]) * FOX_SCALE).astype(BF16)
    kf = _dot(hb, wfk_ref[...])
    kf_ref[...] = kf
    kfb_ref[...] = kf.astype(BF16)
    vf = _dot(hb, wfv_ref[...])
    vf_ref[...] = vf
    vfb_ref[...] = vf.astype(BF16)
    z = _dot(hb, wfl_ref[...]) + bfl_ref[...]
    logf = jnp.minimum(z, 0.0) - jnp.log1p(jnp.exp(-jnp.abs(z)))
    lane = lax.broadcasted_iota(jnp.int32, logf.shape, 1)
    logf = jnp.where(lane < H_FOX, logf, 0.0)
    lf_ref[...] = logf[:, :H_FOX]
    cum, augq, augk = _fox_cum_aug(logf, carry_ref[0:1, :], tril3_ref, pq_ref, pk_ref)
    tm = cum.shape[0]
    carry_ref[...] = jnp.broadcast_to(cum[tm - 1:tm, :], carry_ref.shape)
    aq_ref[...] = augq
    ak_ref[...] = augk


def _pre_attention(x, mod8, lw, tabs, consts, cinit, tm):
    B, T, D = x.shape
    nt = T // tm
    N = B * T
    xf = x.reshape(N, D)
    row = lambda w: pl.BlockSpec((tm, w), lambda b, i: (b * nt + i, 0))
    full = lambda a: pl.BlockSpec(a.shape, lambda b, i: (0,) * a.ndim)
    tab = pl.BlockSpec((tm, LANES), lambda b, i: (i, 0))
    weights = [lw["na"], lw["wq"], lw["wkv"], lw["wkr"], lw["wfq"], lw["wfk"], lw["wfv"], lw["wfl"], lw["bfl"],
               lw["qn"], lw["wqu"], lw["kvn"]]
    HD = H_FOX * FOX_DIM
    outs = [(KV_LORA, F32), (ROPE_DIM, F32), (LANES, BF16), (HD, F32), (HD, F32), (H_FOX, F32),
            (H_MLA * HEAD_PAD, BF16), (HD, BF16), (HD, BF16), (HD, BF16), (HD, BF16), (HD, BF16)]
    return pl.pallas_call(
        _pre_kernel,
        out_shape=[jax.ShapeDtypeStruct((N, w), dt) for w, dt in outs],
        grid=(B, nt),
        in_specs=([row(D), pl.BlockSpec((1, 8, D), lambda b, i: (b, 0, 0))] + [full(w) for w in weights]
                  + [tab] * 6 + [full(c) for c in consts]
                  + [pl.BlockSpec((1, 8, LANES), lambda b, i: (b, 0, 0))]),
        out_specs=[row(w) for w, _ in outs],
        scratch_shapes=[pltpu.VMEM((8, LANES), F32)],
        compiler_params=_cparams(("arbitrary", "arbitrary")),
        name="pre_attention",
    )(xf, mod8, *weights, *tabs, *consts, cinit)


def _past_kernel(lf_ref, tril3_ref, pq_ref, pk_ref, ak_ref, last_ref, carry_ref):
    @pl.when(pl.program_id(1) == 0)
    def _():
        carry_ref[...] = jnp.zeros_like(carry_ref)

    cum, _, augk = _fox_cum_aug(lf_ref[...], carry_ref[0:1, :], tril3_ref, pq_ref, pk_ref)
    tm = cum.shape[0]
    carry_ref[...] = jnp.broadcast_to(cum[tm - 1:tm, :], carry_ref.shape)
    ak_ref[...] = augk
    last_ref[0] = carry_ref[...]


def _past_forget(logf_past, consts, tm):
    B, P, H = logf_past.shape
    nt = P // tm
    lf = jnp.pad(logf_past.reshape(B * P, H), ((0, 0), (0, LANES - H)))
    full = lambda a: pl.BlockSpec(a.shape, lambda b, i: (0,) * a.ndim)
    return pl.pallas_call(
        _past_kernel,
        out_shape=[jax.ShapeDtypeStruct((B * P, H_FOX * FOX_DIM), BF16),
                   jax.ShapeDtypeStruct((B, 8, LANES), F32)],
        grid=(B, nt),
        in_specs=[pl.BlockSpec((tm, LANES), lambda b, i: (b * nt + i, 0))] + [full(c) for c in consts],
        out_specs=[pl.BlockSpec((tm, H_FOX * FOX_DIM), lambda b, i: (b * nt + i, 0)),
                   pl.BlockSpec((1, 8, LANES), lambda b, i: (b, 0, 0))],
        scratch_shapes=[pltpu.VMEM((8, LANES), F32)],
        compiler_params=_cparams(("arbitrary", "arbitrary")),
        name="past_forget",
    )(lf, *consts)


def _kvup_kernel(ckv_ref, krp_ref, wk_ref, wv_ref, k_ref, v_ref):
    cb = ckv_ref[...].astype(BF16)
    k_ref[...] = _dot(jnp.concatenate([cb, krp_ref[...]], axis=1), wk_ref[...]).astype(BF16)
    v_ref[...] = _dot(cb, wv_ref[...]).astype(BF16)


def _kv_up(ckv, krp, wk, wv, tm):
    R = ckv.shape[0]
    row = lambda w: pl.BlockSpec((tm, w), lambda i: (i, 0))
    full = lambda a: pl.BlockSpec(a.shape, lambda i: (0,) * a.ndim)
    return pl.pallas_call(
        _kvup_kernel,
        out_shape=[jax.ShapeDtypeStruct((R, H_MLA * HEAD_PAD), BF16),
                   jax.ShapeDtypeStruct((R, H_MLA * V_DIM), BF16)],
        grid=(R // tm,),
        in_specs=[row(KV_LORA), row(LANES), full(wk), full(wv)],
        out_specs=[row(H_MLA * HEAD_PAD), row(H_MLA * V_DIM)],
        compiler_params=_cparams(("arbitrary",)),
        name="mla_kv_up",
    )(ckv, krp, wk, wv)


def _attn_kernel(*refs, fox, tq, tk, p0):
    if fox:
        q_ref, aq_ref, k_ref, ak_ref, v_ref, o_ref, qs_ref, m_ref, l_ref, acc_ref = refs
    else:
        q_ref, k_ref, v_ref, o_ref, m_ref, l_ref, acc_ref = refs
    q0 = p0 + pl.program_id(2) * tq
    nfull = q0 // tk
    nend = (q0 + tq + tk - 1) // tk

    if fox:
        lane = lax.broadcasted_iota(jnp.int32, (tq, LANES), 1)
        q = q_ref[...]
        aq = aq_ref[...]
        zero = jnp.zeros_like(q)
        for j in range(2):
            own = (lane >= j * FOX_DIM) & (lane < (j + 1) * FOX_DIM)
            qs_ref[j] = jnp.concatenate([jnp.where(own, q, zero), jnp.where(own, aq, zero)], axis=1)

    m_ref[...] = jnp.full(m_ref.shape, NEG, F32)
    l_ref[...] = jnp.zeros(l_ref.shape, F32)
    acc_ref[...] = jnp.zeros(acc_ref.shape, F32)

    def step(kt, masked):
        ks = pl.multiple_of(kt * tk, tk)
        v = v_ref[pl.ds(ks, tk), :]
        if fox:
            kj = jnp.concatenate([k_ref[pl.ds(ks, tk), :], ak_ref[pl.ds(ks, tk), :]], axis=1)
        if masked:
            qpos = q0 + lax.broadcasted_iota(jnp.int32, (tq, tk), 0)
            kpos = ks + lax.broadcasted_iota(jnp.int32, (tq, tk), 1)
            if fox:
                ok = kpos <= qpos
            else:
                ok = jnp.right_shift(kpos, CHUNK_SHIFT) <= jnp.right_shift(qpos, CHUNK_SHIFT)
        for j in range(2):
            if fox:
                qj = qs_ref[j]
            else:
                qj = q_ref[:, j * HEAD_PAD:(j + 1) * HEAD_PAD]
                kj = k_ref[pl.ds(ks, tk), j * HEAD_PAD:(j + 1) * HEAD_PAD]
            s = lax.dot_general(qj, kj, (((1,), (1,)), ((), ())), preferred_element_type=F32)
            if masked:
                s = jnp.where(ok, s, NEG)
            m_prev = m_ref[j]
            m_new = jnp.maximum(m_prev, jnp.max(s, axis=1, keepdims=True))
            p = jnp.exp(s - m_new)
            alpha = jnp.exp(m_prev - m_new)
            l_ref[j] = alpha * l_ref[j] + jnp.sum(p, axis=1, keepdims=True)
            acc_ref[j] = alpha * acc_ref[j] + _dot(p.astype(BF16), v)
            m_ref[j] = m_new

    def full_body(kt, c):
        step(kt, False)
        return c

    def mask_body(kt, c):
        step(kt, True)
        return c

    lax.fori_loop(0, nfull, full_body, 0)
    lax.fori_loop(nfull, nend, mask_body, 0)

    lane = lax.broadcasted_iota(jnp.int32, (tq, LANES), 1)
    o0 = acc_ref[0] * (1.0 / l_ref[0])
    o1 = acc_ref[1] * (1.0 / l_ref[1])
    o_ref[...] = jnp.where(lane < V_DIM, o0, o1)


def _attention(qs, ks, v, *, fox, B, T, S, tq, tk, p0):
    nq = T // tq
    H2 = H_MLA // 2
    qw = LANES if fox else 2 * HEAD_PAD
    qspec = pl.BlockSpec((tq, qw), lambda b, h, i: (b * nq + i, h))
    kspec = pl.BlockSpec((S, qw), lambda b, h, i: (b, h))
    vspec = pl.BlockSpec((S, LANES), lambda b, h, i: (b, h))
    if fox:
        in_specs = [qspec, qspec, kspec, kspec, vspec]
        args = (qs[0], qs[1], ks[0], ks[1], v)
        scratch = [pltpu.VMEM((2, tq, 2 * LANES), BF16)]
    else:
        in_specs = [qspec, kspec, vspec]
        args = (qs[0], ks[0], v)
        scratch = []
    scratch += [pltpu.VMEM((2, tq, 1), F32), pltpu.VMEM((2, tq, 1), F32), pltpu.VMEM((2, tq, LANES), F32)]
    return pl.pallas_call(
        functools.partial(_attn_kernel, fox=fox, tq=tq, tk=tk, p0=p0),
        out_shape=jax.ShapeDtypeStruct((B * T, H_MLA * V_DIM), F32),
        grid=(B, H2, nq),
        in_specs=in_specs,
        out_specs=pl.BlockSpec((tq, LANES), lambda b, h, i: (b * nq + i, h)),
        scratch_shapes=scratch,
        compiler_params=_cparams(("arbitrary", "arbitrary", "arbitrary")),
        name="fox_attention" if fox else "mla_attention",
    )(*args)


def _post_kernel(om_ref, of_ref, x_ref, mod_ref, onm_ref, onf_ref, wo_ref, nf_ref, wrh_ref, wrl_ref, br_ref,
                 x1_ref, h2_ref, gate_ref):
    o = jnp.concatenate([_rms(om_ref[...], onm_ref[...]), _rms(of_ref[...], onf_ref[...])], axis=1)
    mix = _dot(o.astype(BF16), wo_ref[...])
    gate1 = mod_ref[0, 2:3, :]
    shift2 = mod_ref[0, 3:4, :]
    scale2 = mod_ref[0, 4:5, :]
    x1 = x_ref[...] + gate1 * mix
    x1_ref[...] = x1
    h2 = _rms(x1, nf_ref[...]) * (1.0 + scale2) + shift2
    hh = h2.astype(BF16)
    h2_ref[...] = hh
    hl = (h2 - hh.astype(F32)).astype(BF16)
    wrh = wrh_ref[...]
    logits = _dot(hh, wrh) + _dot(hl, wrh) + _dot(hh, wrl_ref[...]) + br_ref[...]

    lane = lax.broadcasted_iota(jnp.int32, logits.shape, 1).astype(F32)
    big = float(LANES)
    gmask = lane < N_GROUPS
    gl = jnp.where(gmask, logits, NEG)
    gmax = jnp.max(gl, axis=1, keepdims=True)
    gsum = jnp.sum(jnp.where(gmask, jnp.exp(gl - gmax), 0.0), axis=1, keepdims=True)
    g_p = 1.0 / gsum
    g_idx = jnp.min(jnp.where(gl == gmax, lane, big), axis=1, keepdims=True)
    lo = ROUTER_OFF + g_idx * EXPERTS_PER_GROUP
    emask = (lane >= lo) & (lane < lo + EXPERTS_PER_GROUP)
    el = jnp.where(emask, logits, NEG)
    emax = jnp.max(el, axis=1, keepdims=True)
    ee = jnp.where(emask, jnp.exp(el - emax), 0.0)
    ep = ee / jnp.sum(ee, axis=1, keepdims=True)
    p1 = jnp.max(ep, axis=1, keepdims=True)
    i1 = jnp.min(jnp.where(emask & (ep == p1), lane, big), axis=1, keepdims=True)
    rest = emask & (lane != i1)
    ep2 = jnp.where(rest, ep, -1.0)
    p2 = jnp.max(ep2, axis=1, keepdims=True)
    i2 = jnp.min(jnp.where(rest & (ep2 == p2), lane, big), axis=1, keepdims=True)
    den = p1 + p2
    gate_ref[...] = (jnp.where(lane == i1, g_p * (p1 / den), 0.0)
                     + jnp.where(lane == i2, g_p * (p2 / den), 0.0))


def _post_attention(om, of, x, mod8, lw, tm):
    B, T, D = x.shape
    nt = T // tm
    N = B * T
    row = lambda w: pl.BlockSpec((tm, w), lambda b, i: (b * nt + i, 0))
    full = lambda a: pl.BlockSpec(a.shape, lambda b, i: (0,) * a.ndim)
    weights = [lw["onm"], lw["onf"], lw["wo"], lw["nf"], lw["wrh"], lw["wrl"], lw["br"]]
    HV = H_MLA * V_DIM
    return pl.pallas_call(
        _post_kernel,
        out_shape=[jax.ShapeDtypeStruct((N, D), F32), jax.ShapeDtypeStruct((N, D), BF16),
                   jax.ShapeDtypeStruct((N, LANES), F32)],
        grid=(B, nt),
        in_specs=[row(HV), row(HV), row(D), pl.BlockSpec((1, 8, D), lambda b, i: (b, 0, 0))]
                 + [full(w) for w in weights],
        out_specs=[row(D), row(D), row(LANES)],
        compiler_params=_cparams(("arbitrary", "arbitrary")),
        name="post_attention",
    )(om, of, x.reshape(N, D), mod8, *weights)


def _moe_kernel(h_ref, gate_ref, x1_ref, mod_ref, wg_ref, wu_ref, wd_ref, o_ref, acc_ref):
    g = pl.program_id(1)
    bb, tt, d = x1_ref.shape

    @pl.when(g == 0)
    def _():
        acc_ref[...] = jnp.zeros_like(acc_ref)

    h = h_ref[...]
    gate = gate_ref[...]
    lane = lax.broadcasted_iota(jnp.int32, gate.shape, 1)
    for j in range(EXPERTS_PER_GROUP):
        a = _dot(h, wg_ref[j])
        u = _dot(h, wu_ref[j])
        gcol = jnp.sum(jnp.where(lane == ROUTER_OFF + g * EXPERTS_PER_GROUP + j, gate, 0.0),
                       axis=1, keepdims=True)
        act = (a * (1.0 / (1.0 + jnp.exp(-a)))) * u * gcol
        acc_ref[...] += _dot(act.astype(BF16), wd_ref[j])

    @pl.when(g == N_GROUPS - 1)
    def _():
        gate2 = mod_ref[:, 5:6, :]
        o_ref[...] = x1_ref[...] + gate2 * acc_ref[...].reshape(bb, tt, d)


def _moe(h2, gate, x1, mod8, lw, bb, tt):
    B, T, D = x1.shape
    nt = T // tt
    tm = bb * tt
    row = lambda w: pl.BlockSpec((tm, w), lambda t, g: (t, 0))
    x3 = pl.BlockSpec((bb, tt, D), lambda t, g: (t // nt, t % nt, 0))
    wspec = lambda a: pl.BlockSpec((EXPERTS_PER_GROUP,) + a.shape[1:], lambda t, g: (g, 0, 0))
    return pl.pallas_call(
        _moe_kernel,
        out_shape=jax.ShapeDtypeStruct((B, T, D), F32),
        grid=((B // bb) * nt, N_GROUPS),
        in_specs=[row(D), row(LANES), x3, pl.BlockSpec((bb, 8, D), lambda t, g: (t // nt, 0, 0)),
                  wspec(lw["wg"]), wspec(lw["wu"]), wspec(lw["wd"])],
        out_specs=x3,
        scratch_shapes=[pltpu.VMEM((tm, D), F32)],
        compiler_params=_cparams(("arbitrary", "arbitrary")),
        name="moe_experts",
    )(h2, gate, x1, mod8, lw["wg"], lw["wu"], lw["wd"])


def _norm_kernel(x_ref, g_ref, o_ref):
    o_ref[...] = _rms(x_ref[...], g_ref[...])


def _final_norm(x, g, tm):
    B, T, D = x.shape
    N = B * T
    return pl.pallas_call(
        _norm_kernel,
        out_shape=jax.ShapeDtypeStruct((N, D), F32),
        grid=(N // tm,),
        in_specs=[pl.BlockSpec((tm, D), lambda i: (i, 0)), pl.BlockSpec((1, D), lambda i: (0, 0))],
        out_specs=pl.BlockSpec((tm, D), lambda i: (i, 0)),
        compiler_params=_cparams(("arbitrary",)),
        name="final_norm",
    )(x.reshape(N, D), g.reshape(1, D)).reshape(B, T, D)


def _rope_tables(p0, T):
    half = ROPE_DIM // 2
    inv = ROPE_THETA ** (-jnp.arange(half, dtype=F32) / half)
    ang = (p0 + jnp.arange(T, dtype=jnp.int32)).astype(F32)[:, None] * inv[None, :]
    cos, sin = jnp.cos(ang), jnp.sin(ang)
    z = lambda w: jnp.zeros((T, w), F32)
    one = jnp.ones((T, NOPE_DIM), F32)
    tail = z(HEAD_PAD - NOPE_DIM - ROPE_DIM)
    cq = jnp.concatenate([one, cos, cos, tail], axis=1) * MLA_SCALE
    s1q = jnp.concatenate([z(NOPE_DIM), -sin, z(half), tail], axis=1) * MLA_SCALE
    s2q = jnp.concatenate([z(NOPE_DIM), z(half), sin, tail], axis=1) * MLA_SCALE
    kt = z(LANES - ROPE_DIM)
    ck = jnp.concatenate([cos, cos, kt], axis=1)
    s1k = jnp.concatenate([-sin, z(half), kt], axis=1)
    s2k = jnp.concatenate([z(half), sin, kt], axis=1)
    return cq, s1q, s2q, ck, s1k, s2k


def _layer_weights(l, p):
    D = p["w_in"].shape[1]
    w_in = p["w_in"][l]
    o = np.cumsum([0, Q_LORA, KV_LORA, ROPE_DIM, H_FOX * FOX_DIM, H_FOX * FOX_DIM, H_FOX * FOX_DIM, H_FOX])
    seg = lambda i: w_in[:, o[i]:o[i + 1]]
    padl = lambda w: jnp.pad(w, ((0, 0), (0, LANES - w.shape[1])))
    r1 = lambda v: v.reshape(1, -1).astype(F32)
    wqu = p["w_q_up"][l].reshape(Q_LORA, H_MLA, NOPE_DIM + ROPE_DIM)
    wqu = jnp.pad(wqu, ((0, 0), (0, 0), (0, HEAD_PAD - NOPE_DIM - ROPE_DIM))).reshape(Q_LORA, H_MLA * HEAD_PAD)
    wkv = p["w_kv_up"][l].reshape(KV_LORA, H_MLA, NOPE_DIM + V_DIM)
    wk_c = jnp.pad(wkv[:, :, :NOPE_DIM], ((0, 0), (0, 0), (0, HEAD_PAD - NOPE_DIM)))
    place = np.zeros((LANES, H_MLA, HEAD_PAD), np.float32)
    for j in range(ROPE_DIM):
        place[j, :, NOPE_DIM + j] = 1.0
    wk = jnp.concatenate([wk_c.reshape(KV_LORA, H_MLA * HEAD_PAD),
                          jnp.asarray(place.reshape(LANES, H_MLA * HEAD_PAD))], axis=0)
    wv = wkv[:, :, NOPE_DIM:].reshape(KV_LORA, H_MLA * V_DIM)
    wr = padl(jnp.concatenate([p["w_router_group"][l], p["w_router_expert"][l]], axis=1))
    wrh = wr.astype(BF16)
    wrl = (wr - wrh.astype(F32)).astype(BF16)
    br = padl(jnp.concatenate([p["b_router_group"][l], p["b_router_expert"][l]]).reshape(1, -1)).astype(F32)
    return dict(
        na=r1(p["norm_attn"][l]), wq=seg(0).astype(BF16), wkv=seg(1).astype(BF16), wkr=padl(seg(2)).astype(BF16),
        wfq=seg(3).astype(BF16), wfk=seg(4).astype(BF16), wfv=seg(5).astype(BF16), wfl=padl(seg(6)).astype(BF16),
        bfl=padl(r1(p["b_forget"][l])), qn=r1(p["q_norm"][l]), wqu=wqu.astype(BF16), kvn=r1(p["kv_norm"][l]),
        wk=wk.astype(BF16), wv=wv.astype(BF16),
        onm=r1(p["out_norm_mla"][l]), onf=r1(p["out_norm_fox"][l]), wo=p["w_out"][l].astype(BF16),
        nf=r1(p["norm_ffn"][l]), wrh=wrh, wrl=wrl, br=br,
        wg=p["w_gate"][l].astype(BF16), wu=p["w_up"][l].astype(BF16), wd=p["w_down"][l].astype(BF16),
    )


def _tile(n, pref):
    return pref if n % pref == 0 else n


def _layer(x, mod8, lw, past, cfg):
    B, T, D = x.shape
    tm = cfg["tm"]
    aug_consts = (cfg["tril3"], cfg["pq"], cfg["pk"])
    if past is None:
        P = 0
        cinit = jnp.zeros((B, 8, LANES), F32)
    else:
        P = past[0].shape[1]
        augk_past, cinit = _past_forget(past[4].astype(F32), (cfg["tril3_past"], cfg["pq"], cfg["pk"]),
                                        cfg["tm_past"])
    (ckv, kro, krp, kf, vf, lf, qm, qf, kfb, vfb, aq, ak) = _pre_attention(
        x, mod8, lw, cfg["tabs"], aug_consts, cinit, tm)
    new_rows = (ckv.reshape(B, T, KV_LORA), kro.reshape(B, T, ROPE_DIM),
                kf.reshape(B, T, H_FOX, FOX_DIM), vf.reshape(B, T, H_FOX, FOX_DIM), lf.reshape(B, T, H_FOX))

    S = P + T
    if past is None:
        ckv_all, krp_all, kfb_all, vfb_all, ak_all = ckv, krp, kfb, vfb, ak
        Sp = S
    else:
        Sp = -(-S // cfg["tk"]) * cfg["tk"]
        HD = H_FOX * FOX_DIM

        def cat(old, new, w):
            a = jnp.concatenate([old.reshape(B, P, w).astype(new.dtype), new.reshape(B, T, w)], axis=1)
            return jnp.pad(a, ((0, 0), (0, Sp - S), (0, 0))).reshape(B * Sp, w)

        ckv_all = cat(past[0], ckv, KV_LORA)
        krp_all = cat(jnp.pad(past[1], ((0, 0), (0, 0), (0, LANES - ROPE_DIM))), krp, LANES)
        kfb_all = cat(past[2], kfb, HD)
        vfb_all = cat(past[3], vfb, HD)
        ak_all = cat(augk_past, ak, HD)
    km, vm = _kv_up(ckv_all, krp_all, lw["wk"], lw["wv"], _tile(B * Sp, 512))
    att = dict(B=B, T=T, S=Sp, tq=cfg["tq"], tk=cfg["tk"], p0=P)
    om = _attention((qm,), (km,), vm, fox=False, **att)
    of = _attention((qf, aq), (kfb_all, ak_all), vfb_all, fox=True, **att)
    x1, h2, gate = _post_attention(om, of, x, mod8, lw, tm)
    x2 = _moe(h2, gate, x1.reshape(B, T, D), mod8, lw, cfg["bb"], cfg["tt"])
    return x2, new_rows


def _group_cfg(B, T, P, pq, pk):
    tm = _tile(T, 256)
    tq = _tile(T, 512)
    tk = tq if P == 0 else 128
    tt = _tile(T, 512)
    bb = max(1, min(B, 512 // tt)) if tt == T else 1
    while B % bb:
        bb -= 1
    cfg = dict(tm=tm, tq=tq, tk=tk, tt=tt, bb=bb, tril3=_tril3(tm), pq=pq, pk=pk, tabs=_rope_tables(P, T))
    if P:
        cfg["tm_past"] = _tile(P, 256)
        cfg["tril3_past"] = _tril3(cfg["tm_past"])
    return cfg


def kernel(x_prompt, x_sample, cache_mla_ckv, cache_mla_krope, cache_fox_k, cache_fox_v, cache_fox_logf, c_prompt, c_sample, w_mod, b_mod, norm_attn, w_in, b_forget, q_norm, w_q_up, kv_norm, w_kv_up, out_norm_mla, out_norm_fox, w_out, norm_ffn, w_router_group, b_router_group, w_router_expert, b_router_expert, w_gate, w_up, w_down, final_norm):
    params = dict(norm_attn=norm_attn, w_in=w_in, b_forget=b_forget, q_norm=q_norm, w_q_up=w_q_up,
                  kv_norm=kv_norm, w_kv_up=w_kv_up, out_norm_mla=out_norm_mla, out_norm_fox=out_norm_fox,
                  w_out=w_out, norm_ffn=norm_ffn, w_router_group=w_router_group, b_router_group=b_router_group,
                  w_router_expert=w_router_expert, b_router_expert=b_router_expert,
                  w_gate=w_gate, w_up=w_up, w_down=w_down)
    L = w_mod.shape[0]
    Bp, Tp, D = x_prompt.shape
    Bs, Ts, _ = x_sample.shape
    P = cache_mla_ckv.shape[2]

    R = -(-(Bp + Bs) // 8) * 8
    c_all = jnp.pad(jnp.concatenate([c_prompt, c_sample], axis=0), ((0, R - Bp - Bs), (0, 0)))
    mod = _mod_all(c_all, w_mod, b_mod).reshape(L, R, 6, D)
    mod = jnp.pad(mod, ((0, 0), (0, 0), (0, 2), (0, 0)))

    pq, pk = _aug_placement()
    cfg_p = _group_cfg(Bp, Tp, 0, pq, pk)
    cfg_s = _group_cfg(Bs, Ts, P, pq, pk)

    y_p, y_s = x_prompt, x_sample
    rows_p, rows_s = [], []
    for l in range(L):
        lw = _layer_weights(l, params)
        y_p, r_p = _layer(y_p, mod[l, :Bp], lw, None, cfg_p)
        past = (cache_mla_ckv[l], cache_mla_krope[l], cache_fox_k[l], cache_fox_v[l], cache_fox_logf[l])
        y_s, r_s = _layer(y_s, mod[l, Bp:Bp + Bs], lw, past, cfg_s)
        rows_p.append(r_p)
        rows_s.append(r_s)

    y_prompt = _final_norm(y_p, final_norm, _tile(Bp * Tp, 512))
    y_sample = _final_norm(y_s, final_norm, _tile(Bs * Ts, 512))
    stack = lambda rows, k: jnp.stack([r[k] for r in rows])
    return (y_prompt, y_sample,
            stack(rows_p, 0), stack(rows_p, 1), stack(rows_p, 2), stack(rows_p, 3), stack(rows_p, 4),
            stack(rows_s, 0), stack(rows_s, 1), stack(rows_s, 2), stack(rows_s, 3), stack(rows_s, 4))
```

```python
import functools

import numpy as np
import jax
import jax.numpy as jnp
from jax import lax
from jax.experimental import pallas as pl
from jax.experimental.pallas import tpu as pltpu

CHUNK = 64
CHUNK_SHIFT = 6
EPS = 1e-6
H_MLA = 8
NOPE_DIM = 64
ROPE_DIM = 32
V_DIM = 64
Q_LORA = 384
KV_LORA = 256
ROPE_THETA = 10000.0
MLA_SCALE = (NOPE_DIM + ROPE_DIM) ** -0.5
H_FOX = 8
FOX_DIM = 64
FOX_SCALE = FOX_DIM ** -0.5
N_GROUPS = 4
EXPERTS_PER_GROUP = 8
N_EXPERTS = N_GROUPS * EXPERTS_PER_GROUP
D_EXPERT = 256

LANES = 128
HEAD_PAD = LANES
VT_ROWS = 80
ROUTER_OFF = N_GROUPS
NEG = -1e30
LOG2E = 1.4426950408889634
VMEM_LIMIT = 56 * 1024 * 1024

F32 = jnp.float32
BF16 = jnp.bfloat16


def _cparams(sem):
    return pltpu.CompilerParams(dimension_semantics=sem, vmem_limit_bytes=VMEM_LIMIT)


def _rms(x, g):
    return x * lax.rsqrt(jnp.mean(x * x, axis=-1, keepdims=True) + EPS) * g


def _dot(a, b):
    return jnp.dot(a, b, preferred_element_type=F32)


def _split3(x):
    hi = x.astype(BF16).astype(F32)
    r = x - hi
    mid = r.astype(BF16).astype(F32)
    lo = (r - mid).astype(BF16).astype(F32)
    return hi, mid, lo


def _mod_kernel(c_ref, w_ref, b_ref, o_ref):
    o_ref[0] = _dot(c_ref[...].astype(BF16), w_ref[0].astype(BF16)) + b_ref[0]


def _mod_all(c_all, w_mod, b_mod):
    L, D, N6 = w_mod.shape
    R = c_all.shape[0]
    TN = N6 // 4
    return pl.pallas_call(
        _mod_kernel,
        out_shape=jax.ShapeDtypeStruct((L, R, N6), F32),
        grid=(L, N6 // TN),
        in_specs=[pl.BlockSpec((R, D), lambda l, j: (0, 0)),
                  pl.BlockSpec((1, D, TN), lambda l, j: (l, 0, j)),
                  pl.BlockSpec((1, 1, TN), lambda l, j: (l, 0, j))],
        out_specs=pl.BlockSpec((1, R, TN), lambda l, j: (l, 0, j)),
        compiler_params=_cparams(("arbitrary", "arbitrary")),
        name="adaln_mod",
    )(c_all, w_mod, b_mod.reshape(L, 1, N6))


def _fox_cum_aug(logf, carry_row, tril3_ref, pq_ref, pk_ref):
    hi, mid, lo = _split3(logf)
    stacked = jnp.concatenate([hi.astype(BF16), mid.astype(BF16), lo.astype(BF16)], axis=0)
    cum = _dot(tril3_ref[...], stacked) + carry_row
    lane = lax.broadcasted_iota(jnp.int32, cum.shape, 1)
    cum = jnp.where(lane < H_FOX, cum, 0.0)
    fh, fm, fl = _split3(cum * LOG2E)
    a3 = (fh + pltpu.roll(fm, H_FOX, axis=1) + pltpu.roll(fl, 2 * H_FOX, axis=1)
          + jnp.where(lane == 3 * H_FOX, 1.0, 0.0)).astype(BF16)
    augq = _dot(a3, pq_ref[...]).astype(BF16)
    augk = _dot(a3, pk_ref[...]).astype(BF16)
    return cum, augq, augk


def _aug_placement():
    pq = np.zeros((LANES, H_FOX * FOX_DIM), np.float32)
    pk = np.zeros((LANES, H_FOX * FOX_DIM), np.float32)
    for h in range(H_FOX):
        base = h * FOX_DIM
        for j in range(3):
            pq[3 * H_FOX, base + j] = 1.0
            pq[j * H_FOX + h, base + 3 + j] = 1.0
            pk[j * H_FOX + h, base + j] = -1.0
            pk[3 * H_FOX, base + 3 + j] = 1.0
    return jnp.asarray(pq, BF16), jnp.asarray(pk, BF16)


def _tril3(tm):
    t = np.tril(np.ones((tm, tm), np.float32))
    return jnp.asarray(np.concatenate([t, t, t], axis=1), BF16)


def _rope_lanes(x, c, s1, s2):
    return x * c + pltpu.roll(x, LANES - ROPE_DIM // 2, axis=1) * s1 + pltpu.roll(x, ROPE_DIM // 2, axis=1) * s2


def _pre_kernel(x_ref, mod_ref, na_ref, wq_ref, wkv_ref, wkr_ref, wfq_ref, wfk_ref, wfv_ref, wfl_ref, bfl_ref,
                qn_ref, wqu_ref, kvn_ref, cq_ref, s1q_ref, s2q_ref, ck_ref, s1k_ref, s2k_ref,
                tril3_ref, pq_ref, pk_ref, cinit_ref,
                ckv_ref, kro_ref, krp_ref, kf_ref, vf_ref, lf_ref, qm_ref, qf_ref, kfb_ref, vfb_ref,
                aq_ref, ak_ref, carry_ref):
    @pl.when(pl.program_id(1) == 0)
    def _():
        carry_ref[...] = cinit_ref[0]

    x = x_ref[...]
    shift = mod_ref[0, 0:1, :]
    scale = mod_ref[0, 1:2, :]
    h = _rms(x, na_ref[...]) * (1.0 + scale) + shift
    hb = h.astype(BF16)

    qn = _rms(_dot(hb, wq_ref[...]), qn_ref[...]).astype(BF16)
    q = _dot(qn, wqu_ref[...])
    cq, s1q, s2q = cq_ref[...], s1q_ref[...], s2q_ref[...]
    for hh in range(H_MLA):
        sl = slice(hh * HEAD_PAD, (hh + 1) * HEAD_PAD)
        qm_ref[:, sl] = _rope_lanes(q[:, sl], cq, s1q, s2q).astype(BF16)

    ckv_ref[...] = _rms(_dot(hb, wkv_ref[...]), kvn_ref[...])
    kr = _rope_lanes(_dot(hb, wkr_ref[...]), ck_ref[...], s1k_ref[...], s2k_ref[...])
    kro_ref[...] = kr[:, :ROPE_DIM]
    krp_ref[...] = kr.astype(BF16)

    qf_ref[...] = (_dot(hb, wfq_ref[...]) * (FOX_SCALE * LOG2E)).astype(BF16)
    kf = _dot(hb, wfk_ref[...])
    kf_ref[...] = kf
    kfb_ref[...] = kf.astype(BF16)
    vf = _dot(hb, wfv_ref[...])
    vf_ref[...] = vf
    vfb_ref[...] = vf.astype(BF16)
    z = _dot(hb, wfl_ref[...]) + bfl_ref[...]
    logf = jnp.minimum(z, 0.0) - jnp.log1p(jnp.exp(-jnp.abs(z)))
    lane = lax.broadcasted_iota(jnp.int32, logf.shape, 1)
    logf = jnp.where(lane < H_FOX, logf, 0.0)
    lf_ref[...] = logf[:, :H_FOX]
    cum, augq, augk = _fox_cum_aug(logf, carry_ref[0:1, :], tril3_ref, pq_ref, pk_ref)
    tm = cum.shape[0]
    carry_ref[...] = jnp.broadcast_to(cum[tm - 1:tm, :], carry_ref.shape)
    aq_ref[...] = augq
    ak_ref[...] = augk


def _pre_attention(x, mod8, lw, tabs, consts, cinit, tm):
    B, T, D = x.shape
    nt = T // tm
    N = B * T
    xf = x.reshape(N, D)
    row = lambda w: pl.BlockSpec((tm, w), lambda b, i: (b * nt + i, 0))
    full = lambda a: pl.BlockSpec(a.shape, lambda b, i: (0,) * a.ndim)
    tab = pl.BlockSpec((tm, LANES), lambda b, i: (i, 0))
    weights = [lw["na"], lw["wq"], lw["wkv"], lw["wkr"], lw["wfq"], lw["wfk"], lw["wfv"], lw["wfl"], lw["bfl"],
               lw["qn"], lw["wqu"], lw["kvn"]]
    HD = H_FOX * FOX_DIM
    outs = [(KV_LORA, F32), (ROPE_DIM, F32), (LANES, BF16), (HD, F32), (HD, F32), (H_FOX, F32),
            (H_MLA * HEAD_PAD, BF16), (HD, BF16), (HD, BF16), (HD, BF16), (HD, BF16), (HD, BF16)]
    return pl.pallas_call(
        _pre_kernel,
        out_shape=[jax.ShapeDtypeStruct((N, w), dt) for w, dt in outs],
        grid=(B, nt),
        in_specs=([row(D), pl.BlockSpec((1, 8, D), lambda b, i: (b, 0, 0))] + [full(w) for w in weights]
                  + [tab] * 6 + [full(c) for c in consts]
                  + [pl.BlockSpec((1, 8, LANES), lambda b, i: (b, 0, 0))]),
        out_specs=[row(w) for w, _ in outs],
        scratch_shapes=[pltpu.VMEM((8, LANES), F32)],
        compiler_params=_cparams(("arbitrary", "arbitrary")),
        name="pre_attention",
    )(xf, mod8, *weights, *tabs, *consts, cinit)


def _past_kernel(lf_ref, tril3_ref, pq_ref, pk_ref, ak_ref, last_ref, carry_ref):
    @pl.when(pl.program_id(1) == 0)
    def _():
        carry_ref[...] = jnp.zeros_like(carry_ref)

    cum, _, augk = _fox_cum_aug(lf_ref[...], carry_ref[0:1, :], tril3_ref, pq_ref, pk_ref)
    tm = cum.shape[0]
    carry_ref[...] = jnp.broadcast_to(cum[tm - 1:tm, :], carry_ref.shape)
    ak_ref[...] = augk
    last_ref[0] = carry_ref[...]


def _past_forget(logf_past, consts, tm):
    B, P, H = logf_past.shape
    nt = P // tm
    lf = jnp.pad(logf_past.reshape(B * P, H), ((0, 0), (0, LANES - H)))
    full = lambda a: pl.BlockSpec(a.shape, lambda b, i: (0,) * a.ndim)
    return pl.pallas_call(
        _past_kernel,
        out_shape=[jax.ShapeDtypeStruct((B * P, H_FOX * FOX_DIM), BF16),
                   jax.ShapeDtypeStruct((B, 8, LANES), F32)],
        grid=(B, nt),
        in_specs=[pl.BlockSpec((tm, LANES), lambda b, i: (b * nt + i, 0))] + [full(c) for c in consts],
        out_specs=[pl.BlockSpec((tm, H_FOX * FOX_DIM), lambda b, i: (b * nt + i, 0)),
                   pl.BlockSpec((1, 8, LANES), lambda b, i: (b, 0, 0))],
        scratch_shapes=[pltpu.VMEM((8, LANES), F32)],
        compiler_params=_cparams(("arbitrary", "arbitrary")),
        name="past_forget",
    )(lf, *consts)


def _kvup_kernel(ckv_ref, krp_ref, wk_ref, wv_ref, k_ref, v_ref):
    cb = ckv_ref[...].astype(BF16)
    k_ref[...] = _dot(jnp.concatenate([cb, krp_ref[...]], axis=1), wk_ref[...]).astype(BF16)
    v_ref[...] = _dot(cb, wv_ref[...]).astype(BF16)


def _kv_up(ckv, krp, wk, wv, tm):
    R = ckv.shape[0]
    row = lambda w: pl.BlockSpec((tm, w), lambda i: (i, 0))
    full = lambda a: pl.BlockSpec(a.shape, lambda i: (0,) * a.ndim)
    return pl.pallas_call(
        _kvup_kernel,
        out_shape=[jax.ShapeDtypeStruct((R, H_MLA * HEAD_PAD), BF16),
                   jax.ShapeDtypeStruct((R, H_MLA * V_DIM), BF16)],
        grid=(R // tm,),
        in_specs=[row(KV_LORA), row(LANES), full(wk), full(wv)],
        out_specs=[row(H_MLA * HEAD_PAD), row(H_MLA * V_DIM)],
        compiler_params=_cparams(("arbitrary",)),
        name="mla_kv_up",
    )(ckv, krp, wk, wv)


def _attn_kernel(*refs, fox, tq, tk, p0):
    if fox:
        q_ref, aq_ref, k_ref, ak_ref, vt_ref, o_ref, qs_ref, sa_ref, sb_ref, m_ref, acc_ref = refs
    else:
        q_ref, k_ref, vt_ref, o_ref, sa_ref, sb_ref, m_ref, acc_ref = refs
    q0 = p0 + pl.program_id(2) * tq
    nfull = q0 // tk

    if fox:
        lane = lax.broadcasted_iota(jnp.int32, (tq, LANES), 1)
        q = q_ref[...]
        aq = aq_ref[...]
        zero = jnp.zeros_like(q)
        for j in range(2):
            own = (lane >= j * FOX_DIM) & (lane < (j + 1) * FOX_DIM)
            qs_ref[j] = jnp.concatenate([jnp.where(own, q, zero), jnp.where(own, aq, zero)], axis=1)

    m_ref[...] = jnp.full(m_ref.shape, NEG, F32)
    acc_ref[...] = jnp.zeros(acc_ref.shape, F32)

    def scores(kt, s_ref):
        ks = pl.multiple_of(kt * tk, tk)
        if fox:
            kj = jnp.concatenate([k_ref[pl.ds(ks, tk), :], ak_ref[pl.ds(ks, tk), :]], axis=1)
        for j in range(2):
            if fox:
                qj = qs_ref[j]
            else:
                qj = q_ref[:, j * HEAD_PAD:(j + 1) * HEAD_PAD]
                kj = k_ref[pl.ds(ks, tk), j * HEAD_PAD:(j + 1) * HEAD_PAD]
            s_ref[j] = lax.dot_general(kj, qj, (((1,), (1,)), ((), ())), preferred_element_type=F32)

    def softmax_pv(kt, s_ref, masked):
        ks = pl.multiple_of(kt * tk, tk)
        if masked:
            kpos = ks + lax.broadcasted_iota(jnp.int32, (tk, tq), 0)
            qpos = q0 + lax.broadcasted_iota(jnp.int32, (tk, tq), 1)
            if fox:
                ok = kpos <= qpos
            else:
                ok = jnp.right_shift(kpos, CHUNK_SHIFT) <= jnp.right_shift(qpos, CHUNK_SHIFT)
        for j in range(2):
            s = s_ref[j]
            if masked:
                s = jnp.where(ok, s, NEG)
            m_prev = m_ref[j]
            m_new = jnp.maximum(m_prev, jnp.max(s, axis=0, keepdims=True))
            p = jnp.exp2(s - m_new)
            alpha = jnp.exp2(m_prev - m_new)
            vt = vt_ref[j * VT_ROWS:(j + 1) * VT_ROWS, pl.ds(ks, tk)]
            acc_ref[j] = alpha * acc_ref[j] + _dot(vt, p.astype(BF16))
            m_ref[j] = m_new

    scores(0, sa_ref)

    def pair_body(c, carry):
        kt = 2 * c
        scores(kt + 1, sb_ref)
        softmax_pv(kt, sa_ref, False)
        scores(kt + 2, sa_ref)
        softmax_pv(kt + 1, sb_ref, False)
        return carry

    lax.fori_loop(0, nfull // 2, pair_body, 0)
    odd = (nfull % 2) == 1

    @pl.when(odd)
    def _():
        scores(nfull, sb_ref)
        softmax_pv(nfull - 1, sa_ref, False)
        softmax_pv(nfull, sb_ref, True)

    @pl.when(jnp.logical_not(odd))
    def _():
        softmax_pv(nfull, sa_ref, True)

    outs = []
    for j in range(2):
        acc = acc_ref[j]
        outs.append(acc[:V_DIM] * (1.0 / acc[V_DIM:V_DIM + 1]))
    o_ref[...] = jnp.concatenate(outs, axis=0).T


def _attention(qs, ks, v, *, fox, B, T, S, tq, tk, p0):
    H2 = H_MLA // 2
    T_in = T
    if T < LANES:
        qs = tuple(jnp.pad(q.reshape(B, T, -1), ((0, 0), (0, LANES - T), (0, 0))).reshape(B * LANES, -1)
                   for q in qs)
        T = tq = LANES
    assert tk % tq == 0 and p0 % tk == 0 and S % tk == 0 and S >= p0 + T, (tq, tk, p0, S, T)
    assert tq == tk or T == tq, (tq, tk, T)
    nq = T // tq
    vt = v.reshape(B, S, H_MLA, V_DIM).transpose(0, 2, 3, 1)
    vt = jnp.concatenate([vt, jnp.ones((B, H_MLA, 1, S), BF16),
                          jnp.zeros((B, H_MLA, VT_ROWS - V_DIM - 1, S), BF16)], axis=2)
    vt = vt.reshape(B * H_MLA * VT_ROWS, S)
    qw = LANES if fox else 2 * HEAD_PAD
    qspec = pl.BlockSpec((tq, qw), lambda b, h, i: (b * nq + i, h))
    kspec = pl.BlockSpec((S, qw), lambda b, h, i: (b, h))
    vspec = pl.BlockSpec((2 * VT_ROWS, S), lambda b, h, i: (b * H2 + h, 0))
    if fox:
        in_specs = [qspec, qspec, kspec, kspec, vspec]
        args = (qs[0], qs[1], ks[0], ks[1], vt)
        scratch = [pltpu.VMEM((2, tq, 2 * LANES), BF16)]
    else:
        in_specs = [qspec, kspec, vspec]
        args = (qs[0], ks[0], vt)
        scratch = []
    scratch += [pltpu.VMEM((2, tk, tq), F32), pltpu.VMEM((2, tk, tq), F32),
                pltpu.VMEM((2, 1, tq), F32), pltpu.VMEM((2, VT_ROWS, tq), F32)]
    out = pl.pallas_call(
        functools.partial(_attn_kernel, fox=fox, tq=tq, tk=tk, p0=p0),
        out_shape=jax.ShapeDtypeStruct((B * T, H_MLA * V_DIM), F32),
        grid=(B, H2, nq),
        in_specs=in_specs,
        out_specs=pl.BlockSpec((tq, LANES), lambda b, h, i: (b * nq + i, h)),
        scratch_shapes=scratch,
        compiler_params=_cparams(("arbitrary", "arbitrary", "arbitrary")),
        name="fox_attention" if fox else "mla_attention",
    )(*args)
    if T_in != T:
        out = out.reshape(B, T, -1)[:, :T_in].reshape(B * T_in, -1)
    return out


def _post_kernel(om_ref, of_ref, x_ref, mod_ref, onm_ref, onf_ref, wo_ref, nf_ref, wrh_ref, wrl_ref, br_ref,
                 x1_ref, h2_ref, gate_ref):
    o = jnp.concatenate([_rms(om_ref[...], onm_ref[...]), _rms(of_ref[...], onf_ref[...])], axis=1)
    mix = _dot(o.astype(BF16), wo_ref[...])
    gate1 = mod_ref[0, 2:3, :]
    shift2 = mod_ref[0, 3:4, :]
    scale2 = mod_ref[0, 4:5, :]
    x1 = x_ref[...] + gate1 * mix
    x1_ref[...] = x1
    h2 = _rms(x1, nf_ref[...]) * (1.0 + scale2) + shift2
    hh = h2.astype(BF16)
    h2_ref[...] = hh
    hl = (h2 - hh.astype(F32)).astype(BF16)
    wrh = wrh_ref[...]
    logits = _dot(hh, wrh) + _dot(hl, wrh) + _dot(hh, wrl_ref[...]) + br_ref[...]

    lane = lax.broadcasted_iota(jnp.int32, logits.shape, 1).astype(F32)
    big = float(LANES)
    gmask = lane < N_GROUPS
    gl = jnp.where(gmask, logits, NEG)
    gmax = jnp.max(gl, axis=1, keepdims=True)
    gsum = jnp.sum(jnp.where(gmask, jnp.exp(gl - gmax), 0.0), axis=1, keepdims=True)
    g_p = 1.0 / gsum
    g_idx = jnp.min(jnp.where(gl == gmax, lane, big), axis=1, keepdims=True)
    lo = ROUTER_OFF + g_idx * EXPERTS_PER_GROUP
    emask = (lane >= lo) & (lane < lo + EXPERTS_PER_GROUP)
    el = jnp.where(emask, logits, NEG)
    emax = jnp.max(el, axis=1, keepdims=True)
    ee = jnp.where(emask, jnp.exp(el - emax), 0.0)
    ep = ee / jnp.sum(ee, axis=1, keepdims=True)
    p1 = jnp.max(ep, axis=1, keepdims=True)
    i1 = jnp.min(jnp.where(emask & (ep == p1), lane, big), axis=1, keepdims=True)
    rest = emask & (lane != i1)
    ep2 = jnp.where(rest, ep, -1.0)
    p2 = jnp.max(ep2, axis=1, keepdims=True)
    i2 = jnp.min(jnp.where(rest & (ep2 == p2), lane, big), axis=1, keepdims=True)
    den = p1 + p2
    gate_ref[...] = (jnp.where(lane == i1, g_p * (p1 / den), 0.0)
                     + jnp.where(lane == i2, g_p * (p2 / den), 0.0))


def _post_attention(om, of, x, mod8, lw, tm):
    B, T, D = x.shape
    nt = T // tm
    N = B * T
    row = lambda w: pl.BlockSpec((tm, w), lambda b, i: (b * nt + i, 0))
    full = lambda a: pl.BlockSpec(a.shape, lambda b, i: (0,) * a.ndim)
    weights = [lw["onm"], lw["onf"], lw["wo"], lw["nf"], lw["wrh"], lw["wrl"], lw["br"]]
    HV = H_MLA * V_DIM
    return pl.pallas_call(
        _post_kernel,
        out_shape=[jax.ShapeDtypeStruct((N, D), F32), jax.ShapeDtypeStruct((N, D), BF16),
                   jax.ShapeDtypeStruct((N, LANES), F32)],
        grid=(B, nt),
        in_specs=[row(HV), row(HV), row(D), pl.BlockSpec((1, 8, D), lambda b, i: (b, 0, 0))]
                 + [full(w) for w in weights],
        out_specs=[row(D), row(D), row(LANES)],
        compiler_params=_cparams(("arbitrary", "arbitrary")),
        name="post_attention",
    )(om, of, x.reshape(N, D), mod8, *weights)


def _moe_kernel(h_ref, gate_ref, x1_ref, mod_ref, wg_ref, wu_ref, wd_ref, o_ref, acc_ref):
    g = pl.program_id(1)
    bb, tt, d = x1_ref.shape

    @pl.when(g == 0)
    def _():
        acc_ref[...] = jnp.zeros_like(acc_ref)

    h = h_ref[...]
    gate = gate_ref[...]
    lane = lax.broadcasted_iota(jnp.int32, gate.shape, 1)
    for j in range(EXPERTS_PER_GROUP):
        a = _dot(h, wg_ref[j])
        u = _dot(h, wu_ref[j])
        gcol = jnp.sum(jnp.where(lane == ROUTER_OFF + g * EXPERTS_PER_GROUP + j, gate, 0.0),
                       axis=1, keepdims=True)
        act = (a * (1.0 / (1.0 + jnp.exp(-a)))) * u * gcol
        acc_ref[...] += _dot(act.astype(BF16), wd_ref[j])

    @pl.when(g == N_GROUPS - 1)
    def _():
        gate2 = mod_ref[:, 5:6, :]
        o_ref[...] = x1_ref[...] + gate2 * acc_ref[...].reshape(bb, tt, d)


def _moe(h2, gate, x1, mod8, lw, bb, tt):
    B, T, D = x1.shape
    nt = T // tt
    tm = bb * tt
    row = lambda w: pl.BlockSpec((tm, w), lambda t, g: (t, 0))
    x3 = pl.BlockSpec((bb, tt, D), lambda t, g: (t // nt, t % nt, 0))
    wspec = lambda a: pl.BlockSpec((EXPERTS_PER_GROUP,) + a.shape[1:], lambda t, g: (g, 0, 0))
    return pl.pallas_call(
        _moe_kernel,
        out_shape=jax.ShapeDtypeStruct((B, T, D), F32),
        grid=((B // bb) * nt, N_GROUPS),
        in_specs=[row(D), row(LANES), x3, pl.BlockSpec((bb, 8, D), lambda t, g: (t // nt, 0, 0)),
                  wspec(lw["wg"]), wspec(lw["wu"]), wspec(lw["wd"])],
        out_specs=x3,
        scratch_shapes=[pltpu.VMEM((tm, D), F32)],
        compiler_params=_cparams(("arbitrary", "arbitrary")),
        name="moe_experts",
    )(h2, gate, x1, mod8, lw["wg"], lw["wu"], lw["wd"])


def _norm_kernel(x_ref, g_ref, o_ref):
    o_ref[...] = _rms(x_ref[...], g_ref[...])


def _final_norm(x, g, tm):
    B, T, D = x.shape
    N = B * T
    return pl.pallas_call(
        _norm_kernel,
        out_shape=jax.ShapeDtypeStruct((N, D), F32),
        grid=(N // tm,),
        in_specs=[pl.BlockSpec((tm, D), lambda i: (i, 0)), pl.BlockSpec((1, D), lambda i: (0, 0))],
        out_specs=pl.BlockSpec((tm, D), lambda i: (i, 0)),
        compiler_params=_cparams(("arbitrary",)),
        name="final_norm",
    )(x.reshape(N, D), g.reshape(1, D)).reshape(B, T, D)


def _rope_tables(p0, T):
    half = ROPE_DIM // 2
    inv = ROPE_THETA ** (-jnp.arange(half, dtype=F32) / half)
    ang = (p0 + jnp.arange(T, dtype=jnp.int32)).astype(F32)[:, None] * inv[None, :]
    cos, sin = jnp.cos(ang), jnp.sin(ang)
    z = lambda w: jnp.zeros((T, w), F32)
    one = jnp.ones((T, NOPE_DIM), F32)
    tail = z(HEAD_PAD - NOPE_DIM - ROPE_DIM)
    qs = MLA_SCALE * LOG2E
    cq = jnp.concatenate([one, cos, cos, tail], axis=1) * qs
    s1q = jnp.concatenate([z(NOPE_DIM), -sin, z(half), tail], axis=1) * qs
    s2q = jnp.concatenate([z(NOPE_DIM), z(half), sin, tail], axis=1) * qs
    kt = z(LANES - ROPE_DIM)
    ck = jnp.concatenate([cos, cos, kt], axis=1)
    s1k = jnp.concatenate([-sin, z(half), kt], axis=1)
    s2k = jnp.concatenate([z(half), sin, kt], axis=1)
    return cq, s1q, s2q, ck, s1k, s2k


def _layer_weights(l, p):
    D = p["w_in"].shape[1]
    w_in = p["w_in"][l]
    o = np.cumsum([0, Q_LORA, KV_LORA, ROPE_DIM, H_FOX * FOX_DIM, H_FOX * FOX_DIM, H_FOX * FOX_DIM, H_FOX])
    seg = lambda i: w_in[:, o[i]:o[i + 1]]
    padl = lambda w: jnp.pad(w, ((0, 0), (0, LANES - w.shape[1])))
    r1 = lambda v: v.reshape(1, -1).astype(F32)
    wqu = p["w_q_up"][l].reshape(Q_LORA, H_MLA, NOPE_DIM + ROPE_DIM)
    wqu = jnp.pad(wqu, ((0, 0), (0, 0), (0, HEAD_PAD - NOPE_DIM - ROPE_DIM))).reshape(Q_LORA, H_MLA * HEAD_PAD)
    wkv = p["w_kv_up"][l].reshape(KV_LORA, H_MLA, NOPE_DIM + V_DIM)
    wk_c = jnp.pad(wkv[:, :, :NOPE_DIM], ((0, 0), (0, 0), (0, HEAD_PAD - NOPE_DIM)))
    place = np.zeros((LANES, H_MLA, HEAD_PAD), np.float32)
    for j in range(ROPE_DIM):
        place[j, :, NOPE_DIM + j] = 1.0
    wk = jnp.concatenate([wk_c.reshape(KV_LORA, H_MLA * HEAD_PAD),
                          jnp.asarray(place.reshape(LANES, H_MLA * HEAD_PAD))], axis=0)
    wv = wkv[:, :, NOPE_DIM:].reshape(KV_LORA, H_MLA * V_DIM)
    wr = padl(jnp.concatenate([p["w_router_group"][l], p["w_router_expert"][l]], axis=1))
    wrh = wr.astype(BF16)
    wrl = (wr - wrh.astype(F32)).astype(BF16)
    br = padl(jnp.concatenate([p["b_router_group"][l], p["b_router_expert"][l]]).reshape(1, -1)).astype(F32)
    return dict(
        na=r1(p["norm_attn"][l]), wq=seg(0).astype(BF16), wkv=seg(1).astype(BF16), wkr=padl(seg(2)).astype(BF16),
        wfq=seg(3).astype(BF16), wfk=seg(4).astype(BF16), wfv=seg(5).astype(BF16), wfl=padl(seg(6)).astype(BF16),
        bfl=padl(r1(p["b_forget"][l])), qn=r1(p["q_norm"][l]), wqu=wqu.astype(BF16), kvn=r1(p["kv_norm"][l]),
        wk=wk.astype(BF16), wv=wv.astype(BF16),
        onm=r1(p["out_norm_mla"][l]), onf=r1(p["out_norm_fox"][l]), wo=p["w_out"][l].astype(BF16),
        nf=r1(p["norm_ffn"][l]), wrh=wrh, wrl=wrl, br=br,
        wg=p["w_gate"][l].astype(BF16), wu=p["w_up"][l].astype(BF16), wd=p["w_down"][l].astype(BF16),
    )


def _tile(n, pref):
    return pref if n % pref == 0 else n


def _layer(x, mod8, lw, past, cfg):
    B, T, D = x.shape
    tm = cfg["tm"]
    aug_consts = (cfg["tril3"], cfg["pq"], cfg["pk"])
    if past is None:
        P = 0
        cinit = jnp.zeros((B, 8, LANES), F32)
    else:
        P = past[0].shape[1]
        augk_past, cinit = _past_forget(past[4].astype(F32), (cfg["tril3_past"], cfg["pq"], cfg["pk"]),
                                        cfg["tm_past"])
    (ckv, kro, krp, kf, vf, lf, qm, qf, kfb, vfb, aq, ak) = _pre_attention(
        x, mod8, lw, cfg["tabs"], aug_consts, cinit, tm)
    new_rows = (ckv.reshape(B, T, KV_LORA), kro.reshape(B, T, ROPE_DIM),
                kf.reshape(B, T, H_FOX, FOX_DIM), vf.reshape(B, T, H_FOX, FOX_DIM), lf.reshape(B, T, H_FOX))

    S = P + T
    if past is None:
        ckv_all, krp_all, kfb_all, vfb_all, ak_all = ckv, krp, kfb, vfb, ak
        Sp = S
    else:
        Sp = -(-S // cfg["tk"]) * cfg["tk"]
        HD = H_FOX * FOX_DIM

        def cat(old, new, w):
            a = jnp.concatenate([old.reshape(B, P, w).astype(new.dtype), new.reshape(B, T, w)], axis=1)
            return jnp.pad(a, ((0, 0), (0, Sp - S), (0, 0))).reshape(B * Sp, w)

        ckv_all = cat(past[0], ckv, KV_LORA)
        krp_all = cat(jnp.pad(past[1], ((0, 0), (0, 0), (0, LANES - ROPE_DIM))), krp, LANES)
        kfb_all = cat(past[2], kfb, HD)
        vfb_all = cat(past[3], vfb, HD)
        ak_all = cat(augk_past, ak, HD)
    km, vm = _kv_up(ckv_all, krp_all, lw["wk"], lw["wv"], _tile(B * Sp, 512))
    att = dict(B=B, T=T, S=Sp, tq=cfg["tq"], tk=cfg["tk"], p0=P)
    om = _attention((qm,), (km,), vm, fox=False, **att)
    of = _attention((qf, aq), (kfb_all, ak_all), vfb_all, fox=True, **att)
    x1, h2, gate = _post_attention(om, of, x, mod8, lw, tm)
    x2 = _moe(h2, gate, x1.reshape(B, T, D), mod8, lw, cfg["bb"], cfg["tt"])
    return x2, new_rows


def _group_cfg(B, T, P, pq, pk):
    tm = _tile(T, 256)
    tq = _tile(T, 512)
    tk = tq if P == 0 else (512 if P % 512 == 0 else LANES)
    tt = _tile(T, 512)
    bb = max(1, min(B, 512 // tt)) if tt == T else 1
    while B % bb:
        bb -= 1
    cfg = dict(tm=tm, tq=tq, tk=tk, tt=tt, bb=bb, tril3=_tril3(tm), pq=pq, pk=pk, tabs=_rope_tables(P, T))
    if P:
        cfg["tm_past"] = _tile(P, 256)
        cfg["tril3_past"] = _tril3(cfg["tm_past"])
    return cfg


def kernel(x_prompt, x_sample, cache_mla_ckv, cache_mla_krope, cache_fox_k, cache_fox_v, cache_fox_logf, c_prompt, c_sample, w_mod, b_mod, norm_attn, w_in, b_forget, q_norm, w_q_up, kv_norm, w_kv_up, out_norm_mla, out_norm_fox, w_out, norm_ffn, w_router_group, b_router_group, w_router_expert, b_router_expert, w_gate, w_up, w_down, final_norm):
    params = dict(norm_attn=norm_attn, w_in=w_in, b_forget=b_forget, q_norm=q_norm, w_q_up=w_q_up,
                  kv_norm=kv_norm, w_kv_up=w_kv_up, out_norm_mla=out_norm_mla, out_norm_fox=out_norm_fox,
                  w_out=w_out, norm_ffn=norm_ffn, w_router_group=w_router_group, b_router_group=b_router_group,
                  w_router_expert=w_router_expert, b_router_expert=b_router_expert,
                  w_gate=w_gate, w_up=w_up, w_down=w_down)
    L = w_mod.shape[0]
    Bp, Tp, D = x_prompt.shape
    Bs, Ts, _ = x_sample.shape
    P = cache_mla_ckv.shape[2]

    R = -(-(Bp + Bs) // 8) * 8
    c_all = jnp.pad(jnp.concatenate([c_prompt, c_sample], axis=0), ((0, R - Bp - Bs), (0, 0)))
    mod = _mod_all(c_all, w_mod, b_mod).reshape(L, R, 6, D)
    mod = jnp.pad(mod, ((0, 0), (0, 0), (0, 2), (0, 0)))

    pq, pk = _aug_placement()
    cfg_p = _group_cfg(Bp, Tp, 0, pq, pk)
    cfg_s = _group_cfg(Bs, Ts, P, pq, pk)

    y_p, y_s = x_prompt, x_sample
    rows_p, rows_s = [], []
    for l in range(L):
        lw = _layer_weights(l, params)
        y_p, r_p = _layer(y_p, mod[l, :Bp], lw, None, cfg_p)
        past = (cache_mla_ckv[l], cache_mla_krope[l], cache_fox_k[l], cache_fox_v[l], cache_fox_logf[l])
        y_s, r_s = _layer(y_s, mod[l, Bp:Bp + Bs], lw, past, cfg_s)
        rows_p.append(r_p)
        rows_s.append(r_s)

    y_prompt = _final_norm(y_p, final_norm, _tile(Bp * Tp, 512))
    y_sample = _final_norm(y_s, final_norm, _tile(Bs * Ts, 512))
    stack = lambda rows, k: jnp.stack([r[k] for r in rows])
    return (y_prompt, y_sample,
            stack(rows_p, 0), stack(rows_p, 1), stack(rows_p, 2), stack(rows_p, 3), stack(rows_p, 4),
            stack(rows_s, 0), stack(rows_s, 1), stack(rows_s, 2), stack(rows_s, 3), stack(rows_s, 4))
```

```python
import functools

import numpy as np
import jax
import jax.numpy as jnp
from jax import lax
from jax.experimental import pallas as pl
from jax.experimental.pallas import tpu as pltpu

CHUNK = 64
CHUNK_SHIFT = 6
EPS = 1e-6
H_MLA = 8
NOPE_DIM = 64
ROPE_DIM = 32
V_DIM = 64
Q_LORA = 384
KV_LORA = 256
ROPE_THETA = 10000.0
MLA_SCALE = (NOPE_DIM + ROPE_DIM) ** -0.5
H_FOX = 8
FOX_DIM = 64
FOX_SCALE = FOX_DIM ** -0.5
N_GROUPS = 4
EXPERTS_PER_GROUP = 8
N_EXPERTS = N_GROUPS * EXPERTS_PER_GROUP
D_EXPERT = 256

LANES = 128
HEAD_PAD = LANES
VT_ROWS = 80
ROUTER_OFF = N_GROUPS
GIDX_LANE = 16
NEG = -1e30
LOG2E = 1.4426950408889634
VMEM_LIMIT = 56 * 1024 * 1024

F32 = jnp.float32
BF16 = jnp.bfloat16


def _cparams(sem):
    return pltpu.CompilerParams(dimension_semantics=sem, vmem_limit_bytes=VMEM_LIMIT)


def _rms(x, g):
    return x * lax.rsqrt(jnp.mean(x * x, axis=-1, keepdims=True) + EPS) * g


def _dot(a, b):
    return jnp.dot(a, b, preferred_element_type=F32)


def _split3(x):
    hi = x.astype(BF16).astype(F32)
    r = x - hi
    mid = r.astype(BF16).astype(F32)
    lo = (r - mid).astype(BF16).astype(F32)
    return hi, mid, lo


def _vt_rows(vt):
    n = vt.shape[1]
    r = lax.broadcasted_iota(jnp.int32, (VT_ROWS - V_DIM, n), 0)
    tail = jnp.where(r == 0, 1.0, 0.0).astype(BF16)
    vb = vt.astype(BF16)
    parts = []
    for h in range(H_MLA):
        parts += [vb[h * V_DIM:(h + 1) * V_DIM], tail]
    return jnp.concatenate(parts, axis=0)


def _nt_dot(a, b):
    return lax.dot_general(a, b, (((1,), (1,)), ((), ())), preferred_element_type=F32)


def _mod_kernel(c_ref, w_ref, b_ref, o_ref):
    o_ref[0] = _dot(c_ref[...].astype(BF16), w_ref[0].astype(BF16)) + b_ref[0]


def _mod_all(c_all, w_mod, b_mod):
    L, D, N6 = w_mod.shape
    R = c_all.shape[0]
    TN = N6 // 4
    return pl.pallas_call(
        _mod_kernel,
        out_shape=jax.ShapeDtypeStruct((L, R, N6), F32),
        grid=(L, N6 // TN),
        in_specs=[pl.BlockSpec((R, D), lambda l, j: (0, 0)),
                  pl.BlockSpec((1, D, TN), lambda l, j: (l, 0, j)),
                  pl.BlockSpec((1, 1, TN), lambda l, j: (l, 0, j))],
        out_specs=pl.BlockSpec((1, R, TN), lambda l, j: (l, 0, j)),
        compiler_params=_cparams(("arbitrary", "arbitrary")),
        name="adaln_mod",
    )(c_all, w_mod, b_mod.reshape(L, 1, N6))


def _fox_cum_aug(logf, carry_row, tril3_ref, pq_ref, pk_ref):
    hi, mid, lo = _split3(logf)
    stacked = jnp.concatenate([hi.astype(BF16), mid.astype(BF16), lo.astype(BF16)], axis=0)
    cum = _dot(tril3_ref[...], stacked) + carry_row
    lane = lax.broadcasted_iota(jnp.int32, cum.shape, 1)
    cum = jnp.where(lane < H_FOX, cum, 0.0)
    fh, fm, fl = _split3(cum * LOG2E)
    a3 = (fh + pltpu.roll(fm, H_FOX, axis=1) + pltpu.roll(fl, 2 * H_FOX, axis=1)
          + jnp.where(lane == 3 * H_FOX, 1.0, 0.0)).astype(BF16)
    augq = _dot(a3, pq_ref[...]).astype(BF16)
    augk = _dot(a3, pk_ref[...]).astype(BF16)
    return cum, augq, augk


def _aug_placement():
    pq = np.zeros((LANES, H_FOX * FOX_DIM), np.float32)
    pk = np.zeros((LANES, H_FOX * FOX_DIM), np.float32)
    for h in range(H_FOX):
        base = h * FOX_DIM
        for j in range(3):
            pq[3 * H_FOX, base + j] = 1.0
            pq[j * H_FOX + h, base + 3 + j] = 1.0
            pk[j * H_FOX + h, base + j] = -1.0
            pk[3 * H_FOX, base + 3 + j] = 1.0
    return jnp.asarray(pq, BF16), jnp.asarray(pk, BF16)


def _tril3(tm):
    t = np.tril(np.ones((tm, tm), np.float32))
    return jnp.asarray(np.concatenate([t, t, t], axis=1), BF16)


def _rope_lanes(x, c, s1, s2):
    return x * c + pltpu.roll(x, LANES - ROPE_DIM // 2, axis=1) * s1 + pltpu.roll(x, ROPE_DIM // 2, axis=1) * s2


def _pre_kernel(x_ref, mod_ref, na_ref, wq_ref, wkv_ref, wkr_ref, wfq_ref, wfk_ref, wfv_ref, wfl_ref, bfl_ref,
                qn_ref, wqu_ref, kvn_ref, wfvt_ref, cq_ref, s1q_ref, s2q_ref, ck_ref, s1k_ref, s2k_ref,
                tril3_ref, pq_ref, pk_ref, cinit_ref,
                ckv_ref, kro_ref, krp_ref, kf_ref, vf_ref, lf_ref, qm_ref, qf_ref, kfb_ref, vfb_ref,
                aq_ref, ak_ref, vft_ref, carry_ref):
    @pl.when(pl.program_id(1) == 0)
    def _():
        carry_ref[...] = cinit_ref[0]

    x = x_ref[...]
    shift = mod_ref[0, 0:1, :]
    scale = mod_ref[0, 1:2, :]
    h = _rms(x, na_ref[...]) * (1.0 + scale) + shift
    hb = h.astype(BF16)

    qn = _rms(_dot(hb, wq_ref[...]), qn_ref[...]).astype(BF16)
    q = _dot(qn, wqu_ref[...])
    cq, s1q, s2q = cq_ref[...], s1q_ref[...], s2q_ref[...]
    for hh in range(H_MLA):
        sl = slice(hh * HEAD_PAD, (hh + 1) * HEAD_PAD)
        qm_ref[:, sl] = _rope_lanes(q[:, sl], cq, s1q, s2q).astype(BF16)

    ckv_ref[...] = _rms(_dot(hb, wkv_ref[...]), kvn_ref[...])
    kr = _rope_lanes(_dot(hb, wkr_ref[...]), ck_ref[...], s1k_ref[...], s2k_ref[...])
    kro_ref[...] = kr[:, :ROPE_DIM]
    krp_ref[...] = kr.astype(BF16)

    qf_ref[...] = (_dot(hb, wfq_ref[...]) * (FOX_SCALE * LOG2E)).astype(BF16)
    kf = _dot(hb, wfk_ref[...])
    kf_ref[...] = kf
    kfb_ref[...] = kf.astype(BF16)
    vf = _dot(hb, wfv_ref[...])
    vf_ref[...] = vf
    vfb_ref[...] = vf.astype(BF16)
    vft_ref[...] = _vt_rows(_nt_dot(wfvt_ref[...], hb))
    z = _dot(hb, wfl_ref[...]) + bfl_ref[...]
    logf = jnp.minimum(z, 0.0) - jnp.log1p(jnp.exp(-jnp.abs(z)))
    lane = lax.broadcasted_iota(jnp.int32, logf.shape, 1)
    logf = jnp.where(lane < H_FOX, logf, 0.0)
    lf_ref[...] = logf[:, :H_FOX]
    cum, augq, augk = _fox_cum_aug(logf, carry_ref[0:1, :], tril3_ref, pq_ref, pk_ref)
    tm = cum.shape[0]
    carry_ref[...] = jnp.broadcast_to(cum[tm - 1:tm, :], carry_ref.shape)
    aq_ref[...] = augq
    ak_ref[...] = augk


def _pre_attention(x, mod8, lw, tabs, consts, cinit, tm):
    B, T, D = x.shape
    nt = T // tm
    N = B * T
    xf = x.reshape(N, D)
    row = lambda w: pl.BlockSpec((tm, w), lambda b, i: (b * nt + i, 0))
    full = lambda a: pl.BlockSpec(a.shape, lambda b, i: (0,) * a.ndim)
    tab = pl.BlockSpec((tm, LANES), lambda b, i: (i, 0))
    weights = [lw["na"], lw["wq"], lw["wkv"], lw["wkr"], lw["wfq"], lw["wfk"], lw["wfv"], lw["wfl"], lw["bfl"],
               lw["qn"], lw["wqu"], lw["kvn"], lw["wfvt"]]
    HD = H_FOX * FOX_DIM
    outs = [(KV_LORA, F32), (ROPE_DIM, F32), (LANES, BF16), (HD, F32), (HD, F32), (H_FOX, F32),
            (H_MLA * HEAD_PAD, BF16), (HD, BF16), (HD, BF16), (HD, BF16), (HD, BF16), (HD, BF16)]
    VR = H_FOX * VT_ROWS
    return pl.pallas_call(
        _pre_kernel,
        out_shape=[jax.ShapeDtypeStruct((N, w), dt) for w, dt in outs] + [jax.ShapeDtypeStruct((B * VR, T), BF16)],
        grid=(B, nt),
        in_specs=([row(D), pl.BlockSpec((1, 8, D), lambda b, i: (b, 0, 0))] + [full(w) for w in weights]
                  + [tab] * 6 + [full(c) for c in consts]
                  + [pl.BlockSpec((1, 8, LANES), lambda b, i: (b, 0, 0))]),
        out_specs=[row(w) for w, _ in outs] + [pl.BlockSpec((VR, tm), lambda b, i: (b, i))],
        scratch_shapes=[pltpu.VMEM((8, LANES), F32)],
        compiler_params=_cparams(("arbitrary", "arbitrary")),
        name="pre_attention",
    )(xf, mod8, *weights, *tabs, *consts, cinit)


def _past_kernel(lf_ref, tril3_ref, pq_ref, pk_ref, ak_ref, last_ref, carry_ref):
    @pl.when(pl.program_id(1) == 0)
    def _():
        carry_ref[...] = jnp.zeros_like(carry_ref)

    cum, _, augk = _fox_cum_aug(lf_ref[...], carry_ref[0:1, :], tril3_ref, pq_ref, pk_ref)
    tm = cum.shape[0]
    carry_ref[...] = jnp.broadcast_to(cum[tm - 1:tm, :], carry_ref.shape)
    ak_ref[...] = augk
    last_ref[0] = carry_ref[...]


def _past_forget(logf_past, consts, tm):
    B, P, H = logf_past.shape
    nt = P // tm
    lf = jnp.pad(logf_past.reshape(B * P, H), ((0, 0), (0, LANES - H)))
    full = lambda a: pl.BlockSpec(a.shape, lambda b, i: (0,) * a.ndim)
    return pl.pallas_call(
        _past_kernel,
        out_shape=[jax.ShapeDtypeStruct((B * P, H_FOX * FOX_DIM), BF16),
                   jax.ShapeDtypeStruct((B, 8, LANES), F32)],
        grid=(B, nt),
        in_specs=[pl.BlockSpec((tm, LANES), lambda b, i: (b * nt + i, 0))] + [full(c) for c in consts],
        out_specs=[pl.BlockSpec((tm, H_FOX * FOX_DIM), lambda b, i: (b * nt + i, 0)),
                   pl.BlockSpec((1, 8, LANES), lambda b, i: (b, 0, 0))],
        scratch_shapes=[pltpu.VMEM((8, LANES), F32)],
        compiler_params=_cparams(("arbitrary", "arbitrary")),
        name="past_forget",
    )(lf, *consts)


def _kvup_kernel(ckv_ref, krp_ref, wk_ref, wvt_ref, k_ref, vt_ref):
    cb = ckv_ref[...].astype(BF16)
    k_ref[...] = _dot(jnp.concatenate([cb, krp_ref[...]], axis=1), wk_ref[...]).astype(BF16)
    vt_ref[...] = _vt_rows(_nt_dot(wvt_ref[...], cb))


def _kv_up(ckv, krp, wk, wvt, B, S, tm):
    nt = S // tm
    row = lambda w: pl.BlockSpec((tm, w), lambda b, i: (b * nt + i, 0))
    full = lambda a: pl.BlockSpec(a.shape, lambda b, i: (0,) * a.ndim)
    VR = H_MLA * VT_ROWS
    return pl.pallas_call(
        _kvup_kernel,
        out_shape=[jax.ShapeDtypeStruct((B * S, H_MLA * HEAD_PAD), BF16),
                   jax.ShapeDtypeStruct((B * VR, S), BF16)],
        grid=(B, nt),
        in_specs=[row(KV_LORA), row(LANES), full(wk), full(wvt)],
        out_specs=[row(H_MLA * HEAD_PAD), pl.BlockSpec((VR, tm), lambda b, i: (b, i))],
        compiler_params=_cparams(("arbitrary", "arbitrary")),
        name="mla_kv_up",
    )(ckv, krp, wk, wvt)


def _attn_kernel(*refs, fox, tq, tk, p0):
    if fox:
        q_ref, aq_ref, k_ref, ak_ref, vt_ref, o_ref, qs_ref, sa_ref, sb_ref, m_ref, acc_ref = refs
    else:
        q_ref, k_ref, vt_ref, o_ref, sa_ref, sb_ref, m_ref, acc_ref = refs
    q0 = p0 + pl.program_id(2) * tq
    nfull = q0 // tk

    if fox:
        lane = lax.broadcasted_iota(jnp.int32, (tq, LANES), 1)
        q = q_ref[...]
        aq = aq_ref[...]
        zero = jnp.zeros_like(q)
        for j in range(2):
            own = (lane >= j * FOX_DIM) & (lane < (j + 1) * FOX_DIM)
            qs_ref[j] = jnp.concatenate([jnp.where(own, q, zero), jnp.where(own, aq, zero)], axis=1)

    m_ref[...] = jnp.full(m_ref.shape, NEG, F32)
    acc_ref[...] = jnp.zeros(acc_ref.shape, F32)

    def scores(kt, s_ref):
        ks = pl.multiple_of(kt * tk, tk)
        if fox:
            kj = jnp.concatenate([k_ref[pl.ds(ks, tk), :], ak_ref[pl.ds(ks, tk), :]], axis=1)
        for j in range(2):
            if fox:
                qj = qs_ref[j]
            else:
                qj = q_ref[:, j * HEAD_PAD:(j + 1) * HEAD_PAD]
                kj = k_ref[pl.ds(ks, tk), j * HEAD_PAD:(j + 1) * HEAD_PAD]
            s_ref[j] = lax.dot_general(kj, qj, (((1,), (1,)), ((), ())), preferred_element_type=F32)

    def softmax_pv(kt, s_ref, masked):
        ks = pl.multiple_of(kt * tk, tk)
        if masked:
            kpos = ks + lax.broadcasted_iota(jnp.int32, (tk, tq), 0)
            qpos = q0 + lax.broadcasted_iota(jnp.int32, (tk, tq), 1)
            if fox:
                ok = kpos <= qpos
            else:
                ok = jnp.right_shift(kpos, CHUNK_SHIFT) <= jnp.right_shift(qpos, CHUNK_SHIFT)
        for j in range(2):
            s = s_ref[j]
            if masked:
                s = jnp.where(ok, s, NEG)
            m_prev = m_ref[j]
            m_new = jnp.maximum(m_prev, jnp.max(s, axis=0, keepdims=True))
            p = jnp.exp2(s - m_new)
            alpha = jnp.exp2(m_prev - m_new)
            vt = vt_ref[j * VT_ROWS:(j + 1) * VT_ROWS, pl.ds(ks, tk)]
            acc_ref[j] = alpha * acc_ref[j] + _dot(vt, p.astype(BF16))
            m_ref[j] = m_new

    scores(0, sa_ref)

    def pair_body(c, carry):
        kt = 2 * c
        scores(kt + 1, sb_ref)
        softmax_pv(kt, sa_ref, False)
        scores(kt + 2, sa_ref)
        softmax_pv(kt + 1, sb_ref, False)
        return carry

    lax.fori_loop(0, nfull // 2, pair_body, 0)
    odd = (nfull % 2) == 1

    @pl.when(odd)
    def _():
        scores(nfull, sb_ref)
        softmax_pv(nfull - 1, sa_ref, False)
        softmax_pv(nfull, sb_ref, True)

    @pl.when(jnp.logical_not(odd))
    def _():
        softmax_pv(nfull, sa_ref, True)

    outs = []
    for j in range(2):
        acc = acc_ref[j]
        outs.append(acc[:V_DIM] * (1.0 / acc[V_DIM:V_DIM + 1]))
    o_ref[...] = jnp.concatenate(outs, axis=0).T


def _vt_from_rows(v, B, S):
    vt = v.reshape(B, S, H_MLA, V_DIM).transpose(0, 2, 3, 1)
    vt = jnp.concatenate([vt, jnp.ones((B, H_MLA, 1, S), BF16),
                          jnp.zeros((B, H_MLA, VT_ROWS - V_DIM - 1, S), BF16)], axis=2)
    return vt.reshape(B * H_MLA * VT_ROWS, S)


def _attention(qs, ks, vt, *, fox, B, T, S, tq, tk, p0):
    H2 = H_MLA // 2
    T_in = T
    if T < LANES:
        qs = tuple(jnp.pad(q.reshape(B, T, -1), ((0, 0), (0, LANES - T), (0, 0))).reshape(B * LANES, -1)
                   for q in qs)
        T = tq = LANES
    assert tk % tq == 0 and p0 % tk == 0 and S % tk == 0 and S >= p0 + T, (tq, tk, p0, S, T)
    assert tq == tk or T == tq, (tq, tk, T)
    nq = T // tq
    qw = LANES if fox else 2 * HEAD_PAD
    qspec = pl.BlockSpec((tq, qw), lambda b, h, i: (b * nq + i, h))
    kspec = pl.BlockSpec((S, qw), lambda b, h, i: (b, h))
    vspec = pl.BlockSpec((2 * VT_ROWS, S), lambda b, h, i: (b * H2 + h, 0))
    if fox:
        in_specs = [qspec, qspec, kspec, kspec, vspec]
        args = (qs[0], qs[1], ks[0], ks[1], vt)
        scratch = [pltpu.VMEM((2, tq, 2 * LANES), BF16)]
    else:
        in_specs = [qspec, kspec, vspec]
        args = (qs[0], ks[0], vt)
        scratch = []
    scratch += [pltpu.VMEM((2, tk, tq), F32), pltpu.VMEM((2, tk, tq), F32),
                pltpu.VMEM((2, 1, tq), F32), pltpu.VMEM((2, VT_ROWS, tq), F32)]
    out = pl.pallas_call(
        functools.partial(_attn_kernel, fox=fox, tq=tq, tk=tk, p0=p0),
        out_shape=jax.ShapeDtypeStruct((B * T, H_MLA * V_DIM), F32),
        grid=(B, H2, nq),
        in_specs=in_specs,
        out_specs=pl.BlockSpec((tq, LANES), lambda b, h, i: (b * nq + i, h)),
        scratch_shapes=scratch,
        compiler_params=_cparams(("arbitrary", "arbitrary", "arbitrary")),
        name="fox_attention" if fox else "mla_attention",
    )(*args)
    if T_in != T:
        out = out.reshape(B, T, -1)[:, :T_in].reshape(B * T_in, -1)
    return out


def _post_kernel(om_ref, of_ref, x_ref, mod_ref, onm_ref, onf_ref, wo_ref, nf_ref, wrh_ref, wrl_ref, br_ref,
                 x1_ref, hg_ref):
    o = jnp.concatenate([_rms(om_ref[...], onm_ref[...]), _rms(of_ref[...], onf_ref[...])], axis=1)
    mix = _dot(o.astype(BF16), wo_ref[...])
    gate1 = mod_ref[0, 2:3, :]
    shift2 = mod_ref[0, 3:4, :]
    scale2 = mod_ref[0, 4:5, :]
    x1 = x_ref[...] + gate1 * mix
    x1_ref[...] = x1
    h2 = _rms(x1, nf_ref[...]) * (1.0 + scale2) + shift2
    hh = h2.astype(BF16)
    hg_ref[:, :h2.shape[1]] = hh.astype(F32)
    hl = (h2 - hh.astype(F32)).astype(BF16)
    wrh = wrh_ref[...]
    logits = _dot(hh, wrh) + _dot(hl, wrh) + _dot(hh, wrl_ref[...]) + br_ref[...]

    lane = lax.broadcasted_iota(jnp.int32, logits.shape, 1).astype(F32)
    big = float(LANES)
    gmask = lane < N_GROUPS
    gl = jnp.where(gmask, logits, NEG)
    gmax = jnp.max(gl, axis=1, keepdims=True)
    gsum = jnp.sum(jnp.where(gmask, jnp.exp(gl - gmax), 0.0), axis=1, keepdims=True)
    g_p = 1.0 / gsum
    g_idx = jnp.min(jnp.where(gl == gmax, lane, big), axis=1, keepdims=True)
    lo = ROUTER_OFF + g_idx * EXPERTS_PER_GROUP
    emask = (lane >= lo) & (lane < lo + EXPERTS_PER_GROUP)
    el = jnp.where(emask, logits, NEG)
    emax = jnp.max(el, axis=1, keepdims=True)
    ee = jnp.where(emask, jnp.exp(el - emax), 0.0)
    ep = ee / jnp.sum(ee, axis=1, keepdims=True)
    p1 = jnp.max(ep, axis=1, keepdims=True)
    i1 = jnp.min(jnp.where(emask & (ep == p1), lane, big), axis=1, keepdims=True)
    rest = emask & (lane != i1)
    ep2 = jnp.where(rest, ep, -1.0)
    p2 = jnp.max(ep2, axis=1, keepdims=True)
    i2 = jnp.min(jnp.where(rest & (ep2 == p2), lane, big), axis=1, keepdims=True)
    den = p1 + p2
    gate = (jnp.where(lane == i1, g_p * (p1 / den), 0.0) + jnp.where(lane == i2, g_p * (p2 / den), 0.0))
    g8 = jnp.zeros_like(gate)
    for g in range(N_GROUPS):
        g8 = g8 + pltpu.roll(gate, LANES - (ROUTER_OFF + g * EXPERTS_PER_GROUP), axis=1)
    d = h2.shape[1]
    hg_ref[:, d:] = jnp.where(lane < EXPERTS_PER_GROUP, g8, jnp.where(lane == GIDX_LANE, g_idx, 0.0))


def _post_attention(om, of, x, mod8, lw, tm):
    B, T, D = x.shape
    nt = T // tm
    N = B * T
    row = lambda w: pl.BlockSpec((tm, w), lambda b, i: (b * nt + i, 0))
    full = lambda a: pl.BlockSpec(a.shape, lambda b, i: (0,) * a.ndim)
    weights = [lw["onm"], lw["onf"], lw["wo"], lw["nf"], lw["wrh"], lw["wrl"], lw["br"]]
    HV = H_MLA * V_DIM
    return pl.pallas_call(
        _post_kernel,
        out_shape=[jax.ShapeDtypeStruct((N, D), F32), jax.ShapeDtypeStruct((N, D + LANES), F32)],
        grid=(B, nt),
        in_specs=[row(HV), row(HV), row(D), pl.BlockSpec((1, 8, D), lambda b, i: (b, 0, 0))]
                 + [full(w) for w in weights],
        out_specs=[row(D), row(D + LANES)],
        compiler_params=_cparams(("arbitrary", "arbitrary")),
        name="post_attention",
    )(om, of, x.reshape(N, D), mod8, *weights)


def _moe_kernel(tg_ref, src_ref, nxt_ref, dst_ref, h_hbm, wg_ref, wu_ref, wd_ref, y_hbm,
                hbuf, ybuf, gsem, ssem, *, tm, d):
    t = pl.program_id(0)
    nt = pl.num_programs(0)
    slot = t % 2

    def gather(idx_ref, s, start):
        def body(r, c):
            row = idx_ref[0, 0, r] if start else 0
            cp = pltpu.make_async_copy(h_hbm.at[pl.ds(row, 1), :], hbuf.at[s, pl.ds(r, 1), :], gsem.at[s])
            cp.start() if start else cp.wait()
            return c
        lax.fori_loop(0, tm, body, 0, unroll=8)

    def scatter(s, start):
        def body(r, c):
            row = dst_ref[0, 0, r] if start else 0
            cp = pltpu.make_async_copy(ybuf.at[s, pl.ds(r, 1), :], y_hbm.at[pl.ds(row, 1), :], ssem.at[s])
            cp.start() if start else cp.wait()
            return c
        lax.fori_loop(0, tm, body, 0, unroll=8)

    @pl.when(t == 0)
    def _():
        gather(src_ref, 0, True)

    @pl.when(t + 1 < nt)
    def _():
        gather(nxt_ref, 1 - slot, True)

    gather(src_ref, slot, False)

    @pl.when(t >= 2)
    def _():
        scatter(slot, False)

    rows = hbuf[slot]
    h = rows[:, :d].astype(BF16)
    g8 = rows[:, d:]
    lane = lax.broadcasted_iota(jnp.int32, g8.shape, 1)
    acc = jnp.zeros((tm, d), F32)
    for j in range(EXPERTS_PER_GROUP):
        a = _dot(h, wg_ref[j])
        u = _dot(h, wu_ref[j])
        gcol = jnp.sum(jnp.where(lane == j, g8, 0.0), axis=1, keepdims=True)
        act = (a * (1.0 / (1.0 + jnp.exp(-a)))) * u * gcol
        acc = acc + _dot(act.astype(BF16), wd_ref[j])
    ybuf[slot] = acc
    scatter(slot, True)

    @pl.when(t == nt - 1)
    def _():
        scatter(slot, False)

        @pl.when(nt >= 2)
        def _():
            scatter(1 - slot, False)


def _moe(hg, lw, tm):
    N, W = hg.shape
    d = W - LANES
    npad = N_GROUPS * tm
    R = N + npad
    nt = R // tm
    grp = hg[:, d + GIDX_LANE].astype(jnp.int32)
    onehot = (grp[:, None] == jnp.arange(N_GROUPS, dtype=jnp.int32)[None, :]).astype(jnp.int32)
    counts = jnp.sum(onehot, axis=0)
    padded = ((counts + tm - 1) // tm) * tm
    ends = jnp.cumsum(padded)
    starts = ends - padded
    rank = jnp.sum((jnp.cumsum(onehot, axis=0) - 1) * onehot, axis=1)
    pos = starts[grp] + rank
    src = jnp.zeros((R,), jnp.int32).at[pos].set(jnp.arange(N, dtype=jnp.int32))
    is_pad = jnp.ones((R,), jnp.int32).at[pos].set(0)
    dst = jnp.where(is_pad == 1, N + jnp.cumsum(is_pad) - 1, src)
    tile_first = jnp.arange(nt, dtype=jnp.int32) * tm
    tile_group = jnp.minimum(jnp.sum((tile_first[:, None] >= ends[None, :]).astype(jnp.int32), axis=1),
                             N_GROUPS - 1)
    src3 = src.reshape(nt, 1, tm)
    dst3 = dst.reshape(nt, 1, tm)

    idx = lambda f: pl.BlockSpec((1, 1, tm), f, memory_space=pltpu.SMEM)
    wspec = lambda a: pl.BlockSpec((EXPERTS_PER_GROUP,) + a.shape[1:], lambda t, tg: (tg[t], 0, 0))
    grid_spec = pltpu.PrefetchScalarGridSpec(
        num_scalar_prefetch=1,
        grid=(nt,),
        in_specs=[idx(lambda t, tg: (t, 0, 0)), idx(lambda t, tg: (jnp.minimum(t + 1, nt - 1), 0, 0)),
                  idx(lambda t, tg: (t, 0, 0)), pl.BlockSpec(memory_space=pl.ANY),
                  wspec(lw["wg"]), wspec(lw["wu"]), wspec(lw["wd"])],
        out_specs=pl.BlockSpec(memory_space=pl.ANY),
        scratch_shapes=[pltpu.VMEM((2, tm, W), F32), pltpu.VMEM((2, tm, d), F32),
                        pltpu.SemaphoreType.DMA((2,)), pltpu.SemaphoreType.DMA((2,))],
    )
    y = pl.pallas_call(
        functools.partial(_moe_kernel, tm=tm, d=d),
        out_shape=jax.ShapeDtypeStruct((N + npad, d), F32),
        grid_spec=grid_spec,
        compiler_params=_cparams(("arbitrary",)),
        name="moe_experts",
    )(tile_group, src3, src3, dst3, hg, lw["wg"], lw["wu"], lw["wd"])
    return y


def _residual_kernel(x_ref, y_ref, mod_ref, o_ref):
    o_ref[...] = x_ref[...] + mod_ref[0, 5:6, :] * y_ref[...]


def _residual(x1, y, mod8, tm):
    B = mod8.shape[0]
    N, D = x1.shape
    nt = N // B // tm
    row = pl.BlockSpec((tm, D), lambda b, i: (b * nt + i, 0))
    return pl.pallas_call(
        _residual_kernel,
        out_shape=jax.ShapeDtypeStruct((N, D), F32),
        grid=(B, nt),
        in_specs=[row, row, pl.BlockSpec((1, 8, D), lambda b, i: (b, 0, 0))],
        out_specs=row,
        compiler_params=_cparams(("arbitrary", "arbitrary")),
        name="ffn_residual",
    )(x1, y, mod8)


def _norm_kernel(x_ref, g_ref, o_ref):
    o_ref[...] = _rms(x_ref[...], g_ref[...])


def _final_norm(x, g, tm):
    B, T, D = x.shape
    N = B * T
    return pl.pallas_call(
        _norm_kernel,
        out_shape=jax.ShapeDtypeStruct((N, D), F32),
        grid=(N // tm,),
        in_specs=[pl.BlockSpec((tm, D), lambda i: (i, 0)), pl.BlockSpec((1, D), lambda i: (0, 0))],
        out_specs=pl.BlockSpec((tm, D), lambda i: (i, 0)),
        compiler_params=_cparams(("arbitrary",)),
        name="final_norm",
    )(x.reshape(N, D), g.reshape(1, D)).reshape(B, T, D)


def _rope_tables(p0, T):
    half = ROPE_DIM // 2
    inv = ROPE_THETA ** (-jnp.arange(half, dtype=F32) / half)
    ang = (p0 + jnp.arange(T, dtype=jnp.int32)).astype(F32)[:, None] * inv[None, :]
    cos, sin = jnp.cos(ang), jnp.sin(ang)
    z = lambda w: jnp.zeros((T, w), F32)
    one = jnp.ones((T, NOPE_DIM), F32)
    tail = z(HEAD_PAD - NOPE_DIM - ROPE_DIM)
    qs = MLA_SCALE * LOG2E
    cq = jnp.concatenate([one, cos, cos, tail], axis=1) * qs
    s1q = jnp.concatenate([z(NOPE_DIM), -sin, z(half), tail], axis=1) * qs
    s2q = jnp.concatenate([z(NOPE_DIM), z(half), sin, tail], axis=1) * qs
    kt = z(LANES - ROPE_DIM)
    ck = jnp.concatenate([cos, cos, kt], axis=1)
    s1k = jnp.concatenate([-sin, z(half), kt], axis=1)
    s2k = jnp.concatenate([z(half), sin, kt], axis=1)
    return cq, s1q, s2q, ck, s1k, s2k


def _layer_weights(l, p):
    D = p["w_in"].shape[1]
    w_in = p["w_in"][l]
    o = np.cumsum([0, Q_LORA, KV_LORA, ROPE_DIM, H_FOX * FOX_DIM, H_FOX * FOX_DIM, H_FOX * FOX_DIM, H_FOX])
    seg = lambda i: w_in[:, o[i]:o[i + 1]]
    padl = lambda w: jnp.pad(w, ((0, 0), (0, LANES - w.shape[1])))
    r1 = lambda v: v.reshape(1, -1).astype(F32)
    wqu = p["w_q_up"][l].reshape(Q_LORA, H_MLA, NOPE_DIM + ROPE_DIM)
    wqu = jnp.pad(wqu, ((0, 0), (0, 0), (0, HEAD_PAD - NOPE_DIM - ROPE_DIM))).reshape(Q_LORA, H_MLA * HEAD_PAD)
    wkv = p["w_kv_up"][l].reshape(KV_LORA, H_MLA, NOPE_DIM + V_DIM)
    wk_c = jnp.pad(wkv[:, :, :NOPE_DIM], ((0, 0), (0, 0), (0, HEAD_PAD - NOPE_DIM)))
    place = np.zeros((LANES, H_MLA, HEAD_PAD), np.float32)
    for j in range(ROPE_DIM):
        place[j, :, NOPE_DIM + j] = 1.0
    wk = jnp.concatenate([wk_c.reshape(KV_LORA, H_MLA * HEAD_PAD),
                          jnp.asarray(place.reshape(LANES, H_MLA * HEAD_PAD))], axis=0)
    wv = wkv[:, :, NOPE_DIM:].reshape(KV_LORA, H_MLA * V_DIM)
    wr = padl(jnp.concatenate([p["w_router_group"][l], p["w_router_expert"][l]], axis=1))
    wrh = wr.astype(BF16)
    wrl = (wr - wrh.astype(F32)).astype(BF16)
    br = padl(jnp.concatenate([p["b_router_group"][l], p["b_router_expert"][l]]).reshape(1, -1)).astype(F32)
    return dict(
        na=r1(p["norm_attn"][l]), wq=seg(0).astype(BF16), wkv=seg(1).astype(BF16), wkr=padl(seg(2)).astype(BF16),
        wfq=seg(3).astype(BF16), wfk=seg(4).astype(BF16), wfv=seg(5).astype(BF16), wfl=padl(seg(6)).astype(BF16),
        bfl=padl(r1(p["b_forget"][l])), qn=r1(p["q_norm"][l]), wqu=wqu.astype(BF16), kvn=r1(p["kv_norm"][l]),
        wk=wk.astype(BF16), wvt=wv.T.astype(BF16), wfvt=seg(5).T.astype(BF16),
        onm=r1(p["out_norm_mla"][l]), onf=r1(p["out_norm_fox"][l]), wo=p["w_out"][l].astype(BF16),
        nf=r1(p["norm_ffn"][l]), wrh=wrh, wrl=wrl, br=br,
        wg=p["w_gate"][l].astype(BF16), wu=p["w_up"][l].astype(BF16), wd=p["w_down"][l].astype(BF16),
    )


def _tile(n, pref):
    return pref if n % pref == 0 else n


def _layer(x, mod8, lw, past, cfg):
    B, T, D = x.shape
    tm = cfg["tm"]
    aug_consts = (cfg["tril3"], cfg["pq"], cfg["pk"])
    if past is None:
        P = 0
        cinit = jnp.zeros((B, 8, LANES), F32)
    else:
        P = past[0].shape[1]
        augk_past, cinit = _past_forget(past[4].astype(F32), (cfg["tril3_past"], cfg["pq"], cfg["pk"]),
                                        cfg["tm_past"])
    (ckv, kro, krp, kf, vf, lf, qm, qf, kfb, vfb, aq, ak, vft) = _pre_attention(
        x, mod8, lw, cfg["tabs"], aug_consts, cinit, tm)
    new_rows = (ckv.reshape(B, T, KV_LORA), kro.reshape(B, T, ROPE_DIM),
                kf.reshape(B, T, H_FOX, FOX_DIM), vf.reshape(B, T, H_FOX, FOX_DIM), lf.reshape(B, T, H_FOX))

    S = P + T
    if past is None:
        ckv_all, krp_all, kfb_all, ak_all = ckv, krp, kfb, ak
        Sp = S
    else:
        Sp = -(-S // cfg["tk"]) * cfg["tk"]
        HD = H_FOX * FOX_DIM

        def cat(old, new, w):
            a = jnp.concatenate([old.reshape(B, P, w).astype(new.dtype), new.reshape(B, T, w)], axis=1)
            return jnp.pad(a, ((0, 0), (0, Sp - S), (0, 0))).reshape(B * Sp, w)

        ckv_all = cat(past[0], ckv, KV_LORA)
        krp_all = cat(jnp.pad(past[1], ((0, 0), (0, 0), (0, LANES - ROPE_DIM))), krp, LANES)
        kfb_all = cat(past[2], kfb, HD)
        vft = _vt_from_rows(cat(past[3], vfb, HD), B, Sp)
        ak_all = cat(augk_past, ak, HD)
    km, vmt = _kv_up(ckv_all, krp_all, lw["wk"], lw["wvt"], B, Sp, _tile(Sp, 512))
    att = dict(B=B, T=T, S=Sp, tq=cfg["tq"], tk=cfg["tk"], p0=P)
    om = _attention((qm,), (km,), vmt, fox=False, **att)
    of = _attention((qf, aq), (kfb_all, ak_all), vft, fox=True, **att)
    x1, hg = _post_attention(om, of, x, mod8, lw, tm)
    y = _moe(hg, lw, cfg["tms"])
    x2 = _residual(x1, y, mod8, tm).reshape(B, T, D)
    return x2, new_rows


def _group_cfg(B, T, P, pq, pk):
    tm = _tile(T, 256)
    tq = _tile(T, 512)
    tk = tq if P == 0 else (512 if P % 512 == 0 else LANES)
    cfg = dict(tm=tm, tq=tq, tk=tk, tms=_tile(B * T, 512), tril3=_tril3(tm), pq=pq, pk=pk,
               tabs=_rope_tables(P, T))
    if P:
        cfg["tm_past"] = _tile(P, 256)
        cfg["tril3_past"] = _tril3(cfg["tm_past"])
    return cfg


def kernel(x_prompt, x_sample, cache_mla_ckv, cache_mla_krope, cache_fox_k, cache_fox_v, cache_fox_logf, c_prompt, c_sample, w_mod, b_mod, norm_attn, w_in, b_forget, q_norm, w_q_up, kv_norm, w_kv_up, out_norm_mla, out_norm_fox, w_out, norm_ffn, w_router_group, b_router_group, w_router_expert, b_router_expert, w_gate, w_up, w_down, final_norm):
    params = dict(norm_attn=norm_attn, w_in=w_in, b_forget=b_forget, q_norm=q_norm, w_q_up=w_q_up,
                  kv_norm=kv_norm, w_kv_up=w_kv_up, out_norm_mla=out_norm_mla, out_norm_fox=out_norm_fox,
                  w_out=w_out, norm_ffn=norm_ffn, w_router_group=w_router_group, b_router_group=b_router_group,
                  w_router_expert=w_router_expert, b_router_expert=b_router_expert,
                  w_gate=w_gate, w_up=w_up, w_down=w_down)
    L = w_mod.shape[0]
    Bp, Tp, D = x_prompt.shape
    Bs, Ts, _ = x_sample.shape
    P = cache_mla_ckv.shape[2]

    R = -(-(Bp + Bs) // 8) * 8
    c_all = jnp.pad(jnp.concatenate([c_prompt, c_sample], axis=0), ((0, R - Bp - Bs), (0, 0)))
    mod = _mod_all(c_all, w_mod, b_mod).reshape(L, R, 6, D)
    mod = jnp.pad(mod, ((0, 0), (0, 0), (0, 2), (0, 0)))

    pq, pk = _aug_placement()
    cfg_p = _group_cfg(Bp, Tp, 0, pq, pk)
    cfg_s = _group_cfg(Bs, Ts, P, pq, pk)

    y_p, y_s = x_prompt, x_sample
    rows_p, rows_s = [], []
    for l in range(L):
        lw = _layer_weights(l, params)
        y_p, r_p = _layer(y_p, mod[l, :Bp], lw, None, cfg_p)
        past = (cache_mla_ckv[l], cache_mla_krope[l], cache_fox_k[l], cache_fox_v[l], cache_fox_logf[l])
        y_s, r_s = _layer(y_s, mod[l, Bp:Bp + Bs], lw, past, cfg_s)
        rows_p.append(r_p)
        rows_s.append(r_s)

    y_prompt = _final_norm(y_p, final_norm, _tile(Bp * Tp, 512))
    y_sample = _final_norm(y_s, final_norm, _tile(Bs * Ts, 512))
    stack = lambda rows, k: jnp.stack([r[k] for r in rows])
    return (y_prompt, y_sample,
            stack(rows_p, 0), stack(rows_p, 1), stack(rows_p, 2), stack(rows_p, 3), stack(rows_p, 4),
            stack(rows_s, 0), stack(rows_s, 1), stack(rows_s, 2), stack(rows_s, 3), stack(rows_s, 4))
```

```python
import functools

import numpy as np
import jax
import jax.numpy as jnp
from jax import lax
from jax.experimental import pallas as pl
from jax.experimental.pallas import tpu as pltpu

CHUNK = 64
CHUNK_SHIFT = 6
EPS = 1e-6
H_MLA = 8
NOPE_DIM = 64
ROPE_DIM = 32
V_DIM = 64
Q_LORA = 384
KV_LORA = 256
ROPE_THETA = 10000.0
MLA_SCALE = (NOPE_DIM + ROPE_DIM) ** -0.5
H_FOX = 8
FOX_DIM = 64
FOX_SCALE = FOX_DIM ** -0.5
N_GROUPS = 4
EXPERTS_PER_GROUP = 8
N_EXPERTS = N_GROUPS * EXPERTS_PER_GROUP
D_EXPERT = 256

LANES = 128
HEAD_PAD = LANES
VT_ROWS = 80
ROUTER_OFF = N_GROUPS
GIDX_LANE = 16
NEG = -1e30
LOG2E = 1.4426950408889634
VMEM_LIMIT = 56 * 1024 * 1024

F32 = jnp.float32
BF16 = jnp.bfloat16


def _cparams(sem):
    return pltpu.CompilerParams(dimension_semantics=sem, vmem_limit_bytes=VMEM_LIMIT)


def _rms(x, g):
    return x * lax.rsqrt(jnp.mean(x * x, axis=-1, keepdims=True) + EPS) * g


def _dot(a, b):
    return jnp.dot(a, b, preferred_element_type=F32)


def _split3(x):
    hi = x.astype(BF16).astype(F32)
    r = x - hi
    mid = r.astype(BF16).astype(F32)
    lo = (r - mid).astype(BF16).astype(F32)
    return hi, mid, lo


def _vt_rows(vt):
    n = vt.shape[1]
    r = lax.broadcasted_iota(jnp.int32, (VT_ROWS - V_DIM, n), 0)
    tail = jnp.where(r == 0, 1.0, 0.0).astype(BF16)
    vb = vt.astype(BF16)
    parts = []
    for h in range(H_MLA):
        parts += [vb[h * V_DIM:(h + 1) * V_DIM], tail]
    return jnp.concatenate(parts, axis=0)


def _nt_dot(a, b):
    return lax.dot_general(a, b, (((1,), (1,)), ((), ())), preferred_element_type=F32)


def _mod_kernel(c_ref, w_ref, b_ref, o_ref):
    o_ref[0] = _dot(c_ref[...].astype(BF16), w_ref[0].astype(BF16)) + b_ref[0]


def _mod_all(c_all, w_mod, b_mod):
    L, D, N6 = w_mod.shape
    R = c_all.shape[0]
    TN = N6 // 4
    return pl.pallas_call(
        _mod_kernel,
        out_shape=jax.ShapeDtypeStruct((L, R, N6), F32),
        grid=(L, N6 // TN),
        in_specs=[pl.BlockSpec((R, D), lambda l, j: (0, 0)),
                  pl.BlockSpec((1, D, TN), lambda l, j: (l, 0, j)),
                  pl.BlockSpec((1, 1, TN), lambda l, j: (l, 0, j))],
        out_specs=pl.BlockSpec((1, R, TN), lambda l, j: (l, 0, j)),
        compiler_params=_cparams(("arbitrary", "arbitrary")),
        name="adaln_mod",
    )(c_all, w_mod, b_mod.reshape(L, 1, N6))


def _fox_cum_aug(logf, carry_row, tril3_ref, pq_ref, pk_ref):
    hi, mid, lo = _split3(logf)
    stacked = jnp.concatenate([hi.astype(BF16), mid.astype(BF16), lo.astype(BF16)], axis=0)
    cum = _dot(tril3_ref[...], stacked) + carry_row
    lane = lax.broadcasted_iota(jnp.int32, cum.shape, 1)
    cum = jnp.where(lane < H_FOX, cum, 0.0)
    fh, fm, fl = _split3(cum * LOG2E)
    a3 = (fh + pltpu.roll(fm, H_FOX, axis=1) + pltpu.roll(fl, 2 * H_FOX, axis=1)
          + jnp.where(lane == 3 * H_FOX, 1.0, 0.0)).astype(BF16)
    augq = _dot(a3, pq_ref[...]).astype(BF16)
    augk = _dot(a3, pk_ref[...]).astype(BF16)
    return cum, augq, augk


def _aug_placement():
    pq = np.zeros((LANES, H_FOX * FOX_DIM), np.float32)
    pk = np.zeros((LANES, H_FOX * FOX_DIM), np.float32)
    for h in range(H_FOX):
        base = h * FOX_DIM
        for j in range(3):
            pq[3 * H_FOX, base + j] = 1.0
            pq[j * H_FOX + h, base + 3 + j] = 1.0
            pk[j * H_FOX + h, base + j] = -1.0
            pk[3 * H_FOX, base + 3 + j] = 1.0
    return jnp.asarray(pq, BF16), jnp.asarray(pk, BF16)


def _tril3(tm):
    t = np.tril(np.ones((tm, tm), np.float32))
    return jnp.asarray(np.concatenate([t, t, t], axis=1), BF16)


def _rope_lanes(x, c, s1, s2):
    return x * c + pltpu.roll(x, LANES - ROPE_DIM // 2, axis=1) * s1 + pltpu.roll(x, ROPE_DIM // 2, axis=1) * s2


def _pre_kernel(x_ref, mod_ref, na_ref, wq_ref, wkv_ref, wkr_ref, wfq_ref, wfk_ref, wfv_ref, wfl_ref, bfl_ref,
                qn_ref, wqu_ref, kvn_ref, wfvt_ref, cq_ref, s1q_ref, s2q_ref, ck_ref, s1k_ref, s2k_ref,
                tril3_ref, pq_ref, pk_ref, cinit_ref,
                ckv_ref, kro_ref, krp_ref, kf_ref, vf_ref, lf_ref, qm_ref, qf_ref, kfb_ref, vfb_ref,
                aq_ref, ak_ref, vft_ref, carry_ref):
    @pl.when(pl.program_id(1) == 0)
    def _():
        carry_ref[...] = cinit_ref[0]

    x = x_ref[...]
    shift = mod_ref[0, 0:1, :]
    scale = mod_ref[0, 1:2, :]
    h = _rms(x, na_ref[...]) * (1.0 + scale) + shift
    hb = h.astype(BF16)

    qn = _rms(_dot(hb, wq_ref[...]), qn_ref[...]).astype(BF16)
    q = _dot(qn, wqu_ref[...])
    cq, s1q, s2q = cq_ref[...], s1q_ref[...], s2q_ref[...]
    for hh in range(H_MLA):
        sl = slice(hh * HEAD_PAD, (hh + 1) * HEAD_PAD)
        qm_ref[:, sl] = _rope_lanes(q[:, sl], cq, s1q, s2q).astype(BF16)

    ckv_ref[...] = _rms(_dot(hb, wkv_ref[...]), kvn_ref[...])
    kr = _rope_lanes(_dot(hb, wkr_ref[...]), ck_ref[...], s1k_ref[...], s2k_ref[...])
    kro_ref[...] = kr[:, :ROPE_DIM]
    krp_ref[...] = kr.astype(BF16)

    qf_ref[...] = (_dot(hb, wfq_ref[...]) * (FOX_SCALE * LOG2E)).astype(BF16)
    kf = _dot(hb, wfk_ref[...])
    kf_ref[...] = kf
    kfb_ref[...] = kf.astype(BF16)
    vf = _dot(hb, wfv_ref[...])
    vf_ref[...] = vf
    vfb_ref[...] = vf.astype(BF16)
    vft_ref[...] = _vt_rows(_nt_dot(wfvt_ref[...], hb))
    z = _dot(hb, wfl_ref[...]) + bfl_ref[...]
    logf = jnp.minimum(z, 0.0) - jnp.log1p(jnp.exp(-jnp.abs(z)))
    lane = lax.broadcasted_iota(jnp.int32, logf.shape, 1)
    logf = jnp.where(lane < H_FOX, logf, 0.0)
    lf_ref[...] = logf[:, :H_FOX]
    cum, augq, augk = _fox_cum_aug(logf, carry_ref[0:1, :], tril3_ref, pq_ref, pk_ref)
    tm = cum.shape[0]
    carry_ref[...] = jnp.broadcast_to(cum[tm - 1:tm, :], carry_ref.shape)
    aq_ref[...] = augq
    ak_ref[...] = augk


def _pre_attention(x, mod8, lw, tabs, consts, cinit, tm):
    B, T, D = x.shape
    nt = T // tm
    N = B * T
    xf = x.reshape(N, D)
    row = lambda w: pl.BlockSpec((tm, w), lambda b, i: (b * nt + i, 0))
    full = lambda a: pl.BlockSpec(a.shape, lambda b, i: (0,) * a.ndim)
    tab = pl.BlockSpec((tm, LANES), lambda b, i: (i, 0))
    weights = [lw["na"], lw["wq"], lw["wkv"], lw["wkr"], lw["wfq"], lw["wfk"], lw["wfv"], lw["wfl"], lw["bfl"],
               lw["qn"], lw["wqu"], lw["kvn"], lw["wfvt"]]
    HD = H_FOX * FOX_DIM
    outs = [(KV_LORA, F32), (ROPE_DIM, F32), (LANES, BF16), (HD, F32), (HD, F32), (H_FOX, F32),
            (H_MLA * HEAD_PAD, BF16), (HD, BF16), (HD, BF16), (HD, BF16), (HD, BF16), (HD, BF16)]
    VR = H_FOX * VT_ROWS
    return pl.pallas_call(
        _pre_kernel,
        out_shape=[jax.ShapeDtypeStruct((N, w), dt) for w, dt in outs] + [jax.ShapeDtypeStruct((B * VR, T), BF16)],
        grid=(B, nt),
        in_specs=([row(D), pl.BlockSpec((1, 8, D), lambda b, i: (b, 0, 0))] + [full(w) for w in weights]
                  + [tab] * 6 + [full(c) for c in consts]
                  + [pl.BlockSpec((1, 8, LANES), lambda b, i: (b, 0, 0))]),
        out_specs=[row(w) for w, _ in outs] + [pl.BlockSpec((VR, tm), lambda b, i: (b, i))],
        scratch_shapes=[pltpu.VMEM((8, LANES), F32)],
        compiler_params=_cparams(("arbitrary", "arbitrary")),
        name="pre_attention",
    )(xf, mod8, *weights, *tabs, *consts, cinit)


def _past_kernel(lf_ref, tril3_ref, pq_ref, pk_ref, ak_ref, last_ref, carry_ref):
    @pl.when(pl.program_id(1) == 0)
    def _():
        carry_ref[...] = jnp.zeros_like(carry_ref)

    cum, _, augk = _fox_cum_aug(lf_ref[...], carry_ref[0:1, :], tril3_ref, pq_ref, pk_ref)
    tm = cum.shape[0]
    carry_ref[...] = jnp.broadcast_to(cum[tm - 1:tm, :], carry_ref.shape)
    ak_ref[...] = augk
    last_ref[0] = carry_ref[...]


def _past_forget(logf_past, consts, tm):
    B, P, H = logf_past.shape
    nt = P // tm
    lf = jnp.pad(logf_past.reshape(B * P, H), ((0, 0), (0, LANES - H)))
    full = lambda a: pl.BlockSpec(a.shape, lambda b, i: (0,) * a.ndim)
    return pl.pallas_call(
        _past_kernel,
        out_shape=[jax.ShapeDtypeStruct((B * P, H_FOX * FOX_DIM), BF16),
                   jax.ShapeDtypeStruct((B, 8, LANES), F32)],
        grid=(B, nt),
        in_specs=[pl.BlockSpec((tm, LANES), lambda b, i: (b * nt + i, 0))] + [full(c) for c in consts],
        out_specs=[pl.BlockSpec((tm, H_FOX * FOX_DIM), lambda b, i: (b * nt + i, 0)),
                   pl.BlockSpec((1, 8, LANES), lambda b, i: (b, 0, 0))],
        scratch_shapes=[pltpu.VMEM((8, LANES), F32)],
        compiler_params=_cparams(("arbitrary", "arbitrary")),
        name="past_forget",
    )(lf, *consts)


def _kvup_kernel(ckv_ref, krp_ref, wk_ref, wvt_ref, k_ref, vt_ref):
    cb = ckv_ref[...].astype(BF16)
    k_ref[...] = _dot(jnp.concatenate([cb, krp_ref[...]], axis=1), wk_ref[...]).astype(BF16)
    vt_ref[...] = _vt_rows(_nt_dot(wvt_ref[...], cb))


def _kv_up(ckv, krp, wk, wvt, B, S, tm):
    nt = S // tm
    row = lambda w: pl.BlockSpec((tm, w), lambda b, i: (b * nt + i, 0))
    full = lambda a: pl.BlockSpec(a.shape, lambda b, i: (0,) * a.ndim)
    VR = H_MLA * VT_ROWS
    return pl.pallas_call(
        _kvup_kernel,
        out_shape=[jax.ShapeDtypeStruct((B * S, H_MLA * HEAD_PAD), BF16),
                   jax.ShapeDtypeStruct((B * VR, S), BF16)],
        grid=(B, nt),
        in_specs=[row(KV_LORA), row(LANES), full(wk), full(wvt)],
        out_specs=[row(H_MLA * HEAD_PAD), pl.BlockSpec((VR, tm), lambda b, i: (b, i))],
        compiler_params=_cparams(("arbitrary", "arbitrary")),
        name="mla_kv_up",
    )(ckv, krp, wk, wvt)


def _attn_kernel(*refs, fox, tq, tk, p0):
    if fox:
        q_ref, aq_ref, k_ref, ak_ref, vt_ref, o_ref, qs_ref, sa_ref, sb_ref, m_ref, acc_ref = refs
    else:
        q_ref, k_ref, vt_ref, o_ref, sa_ref, sb_ref, m_ref, acc_ref = refs
    q0 = p0 + pl.program_id(2) * tq
    nfull = q0 // tk

    if fox:
        lane = lax.broadcasted_iota(jnp.int32, (tq, LANES), 1)
        q = q_ref[...]
        aq = aq_ref[...]
        zero = jnp.zeros_like(q)
        for j in range(2):
            own = (lane >= j * FOX_DIM) & (lane < (j + 1) * FOX_DIM)
            qs_ref[j] = jnp.concatenate([jnp.where(own, q, zero), jnp.where(own, aq, zero)], axis=1)

    m_ref[...] = jnp.full(m_ref.shape, NEG, F32)
    acc_ref[...] = jnp.zeros(acc_ref.shape, F32)

    def scores(kt, s_ref):
        ks = pl.multiple_of(kt * tk, tk)
        if fox:
            kj = jnp.concatenate([k_ref[pl.ds(ks, tk), :], ak_ref[pl.ds(ks, tk), :]], axis=1)
        for j in range(2):
            if fox:
                qj = qs_ref[j]
            else:
                qj = q_ref[:, j * HEAD_PAD:(j + 1) * HEAD_PAD]
                kj = k_ref[pl.ds(ks, tk), j * HEAD_PAD:(j + 1) * HEAD_PAD]
            s_ref[j] = lax.dot_general(kj, qj, (((1,), (1,)), ((), ())), preferred_element_type=F32)

    def softmax_pv(kt, s_ref, masked):
        ks = pl.multiple_of(kt * tk, tk)
        if masked:
            kpos = ks + lax.broadcasted_iota(jnp.int32, (tk, tq), 0)
            qpos = q0 + lax.broadcasted_iota(jnp.int32, (tk, tq), 1)
            if fox:
                ok = kpos <= qpos
            else:
                ok = jnp.right_shift(kpos, CHUNK_SHIFT) <= jnp.right_shift(qpos, CHUNK_SHIFT)
        for j in range(2):
            s = s_ref[j]
            if masked:
                s = jnp.where(ok, s, NEG)
            m_prev = m_ref[j]
            m_new = jnp.maximum(m_prev, jnp.max(s, axis=0, keepdims=True))
            p = jnp.exp2(s - m_new)
            alpha = jnp.exp2(m_prev - m_new)
            vt = vt_ref[j * VT_ROWS:(j + 1) * VT_ROWS, pl.ds(ks, tk)]
            acc_ref[j] = alpha * acc_ref[j] + _dot(vt, p.astype(BF16))
            m_ref[j] = m_new

    scores(0, sa_ref)

    def pair_body(c, carry):
        kt = 2 * c
        scores(kt + 1, sb_ref)
        softmax_pv(kt, sa_ref, False)
        scores(kt + 2, sa_ref)
        softmax_pv(kt + 1, sb_ref, False)
        return carry

    lax.fori_loop(0, nfull // 2, pair_body, 0)
    odd = (nfull % 2) == 1

    @pl.when(odd)
    def _():
        scores(nfull, sb_ref)
        softmax_pv(nfull - 1, sa_ref, False)
        softmax_pv(nfull, sb_ref, True)

    @pl.when(jnp.logical_not(odd))
    def _():
        softmax_pv(nfull, sa_ref, True)

    outs = []
    for j in range(2):
        acc = acc_ref[j]
        outs.append(acc[:V_DIM] * (1.0 / acc[V_DIM:V_DIM + 1]))
    o_ref[...] = jnp.concatenate(outs, axis=0).T


def _vt_from_rows(v, B, S):
    vt = v.reshape(B, S, H_MLA, V_DIM).transpose(0, 2, 3, 1)
    vt = jnp.concatenate([vt, jnp.ones((B, H_MLA, 1, S), BF16),
                          jnp.zeros((B, H_MLA, VT_ROWS - V_DIM - 1, S), BF16)], axis=2)
    return vt.reshape(B * H_MLA * VT_ROWS, S)


def _attention(qs, ks, vt, *, fox, B, T, S, tq, tk, p0):
    H2 = H_MLA // 2
    T_in = T
    if T < LANES:
        qs = tuple(jnp.pad(q.reshape(B, T, -1), ((0, 0), (0, LANES - T), (0, 0))).reshape(B * LANES, -1)
                   for q in qs)
        T = tq = LANES
    assert tk % tq == 0 and p0 % tk == 0 and S % tk == 0 and S >= p0 + T, (tq, tk, p0, S, T)
    assert tq == tk or T == tq, (tq, tk, T)
    nq = T // tq
    qw = LANES if fox else 2 * HEAD_PAD
    qspec = pl.BlockSpec((tq, qw), lambda b, h, i: (b * nq + i, h))
    kspec = pl.BlockSpec((S, qw), lambda b, h, i: (b, h))
    vspec = pl.BlockSpec((2 * VT_ROWS, S), lambda b, h, i: (b * H2 + h, 0))
    if fox:
        in_specs = [qspec, qspec, kspec, kspec, vspec]
        args = (qs[0], qs[1], ks[0], ks[1], vt)
        scratch = [pltpu.VMEM((2, tq, 2 * LANES), BF16)]
    else:
        in_specs = [qspec, kspec, vspec]
        args = (qs[0], ks[0], vt)
        scratch = []
    scratch += [pltpu.VMEM((2, tk, tq), F32), pltpu.VMEM((2, tk, tq), F32),
                pltpu.VMEM((2, 1, tq), F32), pltpu.VMEM((2, VT_ROWS, tq), F32)]
    out = pl.pallas_call(
        functools.partial(_attn_kernel, fox=fox, tq=tq, tk=tk, p0=p0),
        out_shape=jax.ShapeDtypeStruct((B * T, H_MLA * V_DIM), F32),
        grid=(B, H2, nq),
        in_specs=in_specs,
        out_specs=pl.BlockSpec((tq, LANES), lambda b, h, i: (b * nq + i, h)),
        scratch_shapes=scratch,
        compiler_params=_cparams(("arbitrary", "arbitrary", "arbitrary")),
        name="fox_attention" if fox else "mla_attention",
    )(*args)
    if T_in != T:
        out = out.reshape(B, T, -1)[:, :T_in].reshape(B * T_in, -1)
    return out


def _post_kernel(om_ref, of_ref, x_ref, mod_ref, onm_ref, onf_ref, wo_ref, nf_ref, wrh_ref, wrl_ref, br_ref,
                 x1_ref, hg_ref):
    o = jnp.concatenate([_rms(om_ref[...], onm_ref[...]), _rms(of_ref[...], onf_ref[...])], axis=1)
    mix = _dot(o.astype(BF16), wo_ref[...])
    gate1 = mod_ref[0, 2:3, :]
    shift2 = mod_ref[0, 3:4, :]
    scale2 = mod_ref[0, 4:5, :]
    x1 = x_ref[...] + gate1 * mix
    x1_ref[...] = x1
    h2 = _rms(x1, nf_ref[...]) * (1.0 + scale2) + shift2
    hh = h2.astype(BF16)
    hg_ref[:, :h2.shape[1]] = hh.astype(F32)
    hl = (h2 - hh.astype(F32)).astype(BF16)
    wrh = wrh_ref[...]
    logits = _dot(hh, wrh) + _dot(hl, wrh) + _dot(hh, wrl_ref[...]) + br_ref[...]

    lane = lax.broadcasted_iota(jnp.int32, logits.shape, 1).astype(F32)
    big = float(LANES)
    gmask = lane < N_GROUPS
    gl = jnp.where(gmask, logits, NEG)
    gmax = jnp.max(gl, axis=1, keepdims=True)
    gsum = jnp.sum(jnp.where(gmask, jnp.exp(gl - gmax), 0.0), axis=1, keepdims=True)
    g_p = 1.0 / gsum
    g_idx = jnp.min(jnp.where(gl == gmax, lane, big), axis=1, keepdims=True)
    lo = ROUTER_OFF + g_idx * EXPERTS_PER_GROUP
    emask = (lane >= lo) & (lane < lo + EXPERTS_PER_GROUP)
    el = jnp.where(emask, logits, NEG)
    emax = jnp.max(el, axis=1, keepdims=True)
    ee = jnp.where(emask, jnp.exp(el - emax), 0.0)
    ep = ee / jnp.sum(ee, axis=1, keepdims=True)
    p1 = jnp.max(ep, axis=1, keepdims=True)
    i1 = jnp.min(jnp.where(emask & (ep == p1), lane, big), axis=1, keepdims=True)
    rest = emask & (lane != i1)
    ep2 = jnp.where(rest, ep, -1.0)
    p2 = jnp.max(ep2, axis=1, keepdims=True)
    i2 = jnp.min(jnp.where(rest & (ep2 == p2), lane, big), axis=1, keepdims=True)
    den = p1 + p2
    gate = (jnp.where(lane == i1, g_p * (p1 / den), 0.0) + jnp.where(lane == i2, g_p * (p2 / den), 0.0))
    g8 = jnp.zeros_like(gate)
    for g in range(N_GROUPS):
        g8 = g8 + pltpu.roll(gate, LANES - (ROUTER_OFF + g * EXPERTS_PER_GROUP), axis=1)
    d = h2.shape[1]
    hg_ref[:, d:] = jnp.where(lane < EXPERTS_PER_GROUP, g8, jnp.where(lane == GIDX_LANE, g_idx, 0.0))


def _post_attention(om, of, x, mod8, lw, tm):
    B, T, D = x.shape
    nt = T // tm
    N = B * T
    row = lambda w: pl.BlockSpec((tm, w), lambda b, i: (b * nt + i, 0))
    full = lambda a: pl.BlockSpec(a.shape, lambda b, i: (0,) * a.ndim)
    weights = [lw["onm"], lw["onf"], lw["wo"], lw["nf"], lw["wrh"], lw["wrl"], lw["br"]]
    HV = H_MLA * V_DIM
    return pl.pallas_call(
        _post_kernel,
        out_shape=[jax.ShapeDtypeStruct((N, D), F32), jax.ShapeDtypeStruct((N, D + LANES), F32)],
        grid=(B, nt),
        in_specs=[row(HV), row(HV), row(D), pl.BlockSpec((1, 8, D), lambda b, i: (b, 0, 0))]
                 + [full(w) for w in weights],
        out_specs=[row(D), row(D + LANES)],
        compiler_params=_cparams(("arbitrary", "arbitrary")),
        name="post_attention",
    )(om, of, x.reshape(N, D), mod8, *weights)


def _row_copies(n, make, start):
    def body(r, c):
        cp = make(r)
        cp.start() if start else cp.wait()
        return c
    lax.fori_loop(0, n, body, 0, unroll=8)


def _dispatch_kernel(pos_ref, h_hbm, hs_in, hs_out, sem, *, tm):
    del hs_in
    t = pl.program_id(0)
    nt = pl.num_programs(0)
    slot = t % 2
    base = t * tm

    def copy(r, s, start):
        src = base + r if start else 0
        dst = pos_ref[0, 0, r] if start else 0
        return pltpu.make_async_copy(h_hbm.at[pl.ds(src, 1), :], hs_out.at[pl.ds(dst, 1), :], sem.at[s])

    _row_copies(tm, lambda r: copy(r, slot, True), True)

    @pl.when(t >= 1)
    def _():
        _row_copies(tm, lambda r: copy(r, 1 - slot, False), False)

    @pl.when(t == nt - 1)
    def _():
        _row_copies(tm, lambda r: copy(r, slot, False), False)


def _dispatch(hg, pos3, R, tm):
    N, W = hg.shape
    return pl.pallas_call(
        functools.partial(_dispatch_kernel, tm=tm),
        out_shape=jax.ShapeDtypeStruct((R, W), F32),
        grid=(N // tm,),
        in_specs=[pl.BlockSpec((1, 1, tm), lambda t: (t, 0, 0), memory_space=pltpu.SMEM),
                  pl.BlockSpec(memory_space=pl.ANY), pl.BlockSpec(memory_space=pl.ANY)],
        out_specs=pl.BlockSpec(memory_space=pl.ANY),
        scratch_shapes=[pltpu.SemaphoreType.DMA((2,))],
        input_output_aliases={2: 0},
        compiler_params=_cparams(("arbitrary",)),
        name="moe_dispatch",
    )(pos3, hg, jnp.zeros((R, W), F32))


def _moe_kernel(tg_ref, hs_ref, wg_ref, wu_ref, wd_ref, ys_ref, *, d):
    del tg_ref
    rows = hs_ref[...]
    h = rows[:, :d].astype(BF16)
    g8 = rows[:, d:]
    lane = lax.broadcasted_iota(jnp.int32, g8.shape, 1)
    acc = jnp.zeros((rows.shape[0], d), F32)
    for j in range(EXPERTS_PER_GROUP):
        a = _dot(h, wg_ref[j])
        u = _dot(h, wu_ref[j])
        gcol = jnp.sum(jnp.where(lane == j, g8, 0.0), axis=1, keepdims=True)
        act = (a * (1.0 / (1.0 + jnp.exp(-a)))) * u * gcol
        acc = acc + _dot(act.astype(BF16), wd_ref[j])
    ys_ref[...] = acc


def _moe(hg, lw, tm, tmd):
    N, W = hg.shape
    d = W - LANES
    R = N + N_GROUPS * tm
    nt = R // tm
    grp = hg[:, d + GIDX_LANE].astype(jnp.int32)
    onehot = (grp[:, None] == jnp.arange(N_GROUPS, dtype=jnp.int32)[None, :]).astype(jnp.int32)
    counts = jnp.sum(onehot, axis=0)
    padded = ((counts + tm - 1) // tm) * tm
    ends = jnp.cumsum(padded)
    starts = ends - padded
    rank = jnp.sum((jnp.cumsum(onehot, axis=0) - 1) * onehot, axis=1)
    pos = jnp.sum(onehot * starts[None, :], axis=1) + rank
    tile_first = jnp.arange(nt, dtype=jnp.int32) * tm
    tile_group = jnp.minimum(jnp.sum((tile_first[:, None] >= ends[None, :]).astype(jnp.int32), axis=1),
                             N_GROUPS - 1)

    hs = _dispatch(hg, pos.reshape(N // tmd, 1, tmd), R, tmd)
    wspec = lambda a: pl.BlockSpec((EXPERTS_PER_GROUP,) + a.shape[1:], lambda t, tg: (tg[t], 0, 0))
    grid_spec = pltpu.PrefetchScalarGridSpec(
        num_scalar_prefetch=1,
        grid=(nt,),
        in_specs=[pl.BlockSpec((tm, W), lambda t, tg: (t, 0)),
                  wspec(lw["wg"]), wspec(lw["wu"]), wspec(lw["wd"])],
        out_specs=pl.BlockSpec((tm, d), lambda t, tg: (t, 0)),
    )
    ys = pl.pallas_call(
        functools.partial(_moe_kernel, d=d),
        out_shape=jax.ShapeDtypeStruct((R, d), F32),
        grid_spec=grid_spec,
        compiler_params=_cparams(("arbitrary",)),
        name="moe_experts",
    )(tile_group, hs, lw["wg"], lw["wu"], lw["wd"])
    return ys, pos


def _residual_kernel(pos_ref, nxt_ref, x_ref, mod_ref, ys_hbm, o_ref, ybuf, sem, *, tm):
    nt = pl.num_programs(1)
    step = pl.program_id(0) * nt + pl.program_id(1)
    total = pl.num_programs(0) * nt
    slot = step % 2

    def copy(idx_ref, r, s, start):
        src = idx_ref[0, 0, r] if start else 0
        return pltpu.make_async_copy(ys_hbm.at[pl.ds(src, 1), :], ybuf.at[s, pl.ds(r, 1), :], sem.at[s])

    @pl.when(step == 0)
    def _():
        _row_copies(tm, lambda r: copy(pos_ref, r, 0, True), True)

    @pl.when(step + 1 < total)
    def _():
        _row_copies(tm, lambda r: copy(nxt_ref, r, 1 - slot, True), True)

    _row_copies(tm, lambda r: copy(pos_ref, r, slot, False), False)
    o_ref[...] = x_ref[...] + mod_ref[0, 5:6, :] * ybuf[slot]


def _residual(x1, ys, pos, mod8, tm):
    B = mod8.shape[0]
    N, D = x1.shape
    nt = N // B // tm
    ns = N // tm
    pos3 = pos.reshape(ns, 1, tm)
    row = pl.BlockSpec((tm, D), lambda b, i: (b * nt + i, 0))
    idx = lambda f: pl.BlockSpec((1, 1, tm), f, memory_space=pltpu.SMEM)
    return pl.pallas_call(
        functools.partial(_residual_kernel, tm=tm),
        out_shape=jax.ShapeDtypeStruct((N, D), F32),
        grid=(B, nt),
        in_specs=[idx(lambda b, i: (b * nt + i, 0, 0)),
                  idx(lambda b, i: (jnp.minimum(b * nt + i + 1, ns - 1), 0, 0)),
                  row, pl.BlockSpec((1, 8, D), lambda b, i: (b, 0, 0)), pl.BlockSpec(memory_space=pl.ANY)],
        out_specs=row,
        scratch_shapes=[pltpu.VMEM((2, tm, D), F32), pltpu.SemaphoreType.DMA((2,))],
        compiler_params=_cparams(("arbitrary", "arbitrary")),
        name="ffn_residual",
    )(pos3, pos3, x1, mod8, ys)


def _norm_kernel(x_ref, g_ref, o_ref):
    o_ref[...] = _rms(x_ref[...], g_ref[...])


def _final_norm(x, g, tm):
    B, T, D = x.shape
    N = B * T
    return pl.pallas_call(
        _norm_kernel,
        out_shape=jax.ShapeDtypeStruct((N, D), F32),
        grid=(N // tm,),
        in_specs=[pl.BlockSpec((tm, D), lambda i: (i, 0)), pl.BlockSpec((1, D), lambda i: (0, 0))],
        out_specs=pl.BlockSpec((tm, D), lambda i: (i, 0)),
        compiler_params=_cparams(("arbitrary",)),
        name="final_norm",
    )(x.reshape(N, D), g.reshape(1, D)).reshape(B, T, D)


def _rope_tables(p0, T):
    half = ROPE_DIM // 2
    inv = ROPE_THETA ** (-jnp.arange(half, dtype=F32) / half)
    ang = (p0 + jnp.arange(T, dtype=jnp.int32)).astype(F32)[:, None] * inv[None, :]
    cos, sin = jnp.cos(ang), jnp.sin(ang)
    z = lambda w: jnp.zeros((T, w), F32)
    one = jnp.ones((T, NOPE_DIM), F32)
    tail = z(HEAD_PAD - NOPE_DIM - ROPE_DIM)
    qs = MLA_SCALE * LOG2E
    cq = jnp.concatenate([one, cos, cos, tail], axis=1) * qs
    s1q = jnp.concatenate([z(NOPE_DIM), -sin, z(half), tail], axis=1) * qs
    s2q = jnp.concatenate([z(NOPE_DIM), z(half), sin, tail], axis=1) * qs
    kt = z(LANES - ROPE_DIM)
    ck = jnp.concatenate([cos, cos, kt], axis=1)
    s1k = jnp.concatenate([-sin, z(half), kt], axis=1)
    s2k = jnp.concatenate([z(half), sin, kt], axis=1)
    return cq, s1q, s2q, ck, s1k, s2k


def _layer_weights(l, p):
    D = p["w_in"].shape[1]
    w_in = p["w_in"][l]
    o = np.cumsum([0, Q_LORA, KV_LORA, ROPE_DIM, H_FOX * FOX_DIM, H_FOX * FOX_DIM, H_FOX * FOX_DIM, H_FOX])
    seg = lambda i: w_in[:, o[i]:o[i + 1]]
    padl = lambda w: jnp.pad(w, ((0, 0), (0, LANES - w.shape[1])))
    r1 = lambda v: v.reshape(1, -1).astype(F32)
    wqu = p["w_q_up"][l].reshape(Q_LORA, H_MLA, NOPE_DIM + ROPE_DIM)
    wqu = jnp.pad(wqu, ((0, 0), (0, 0), (0, HEAD_PAD - NOPE_DIM - ROPE_DIM))).reshape(Q_LORA, H_MLA * HEAD_PAD)
    wkv = p["w_kv_up"][l].reshape(KV_LORA, H_MLA, NOPE_DIM + V_DIM)
    wk_c = jnp.pad(wkv[:, :, :NOPE_DIM], ((0, 0), (0, 0), (0, HEAD_PAD - NOPE_DIM)))
    place = np.zeros((LANES, H_MLA, HEAD_PAD), np.float32)
    for j in range(ROPE_DIM):
        place[j, :, NOPE_DIM + j] = 1.0
    wk = jnp.concatenate([wk_c.reshape(KV_LORA, H_MLA * HEAD_PAD),
                          jnp.asarray(place.reshape(LANES, H_MLA * HEAD_PAD))], axis=0)
    wv = wkv[:, :, NOPE_DIM:].reshape(KV_LORA, H_MLA * V_DIM)
    wr = padl(jnp.concatenate([p["w_router_group"][l], p["w_router_expert"][l]], axis=1))
    wrh = wr.astype(BF16)
    wrl = (wr - wrh.astype(F32)).astype(BF16)
    br = padl(jnp.concatenate([p["b_router_group"][l], p["b_router_expert"][l]]).reshape(1, -1)).astype(F32)
    return dict(
        na=r1(p["norm_attn"][l]), wq=seg(0).astype(BF16), wkv=seg(1).astype(BF16), wkr=padl(seg(2)).astype(BF16),
        wfq=seg(3).astype(BF16), wfk=seg(4).astype(BF16), wfv=seg(5).astype(BF16), wfl=padl(seg(6)).astype(BF16),
        bfl=padl(r1(p["b_forget"][l])), qn=r1(p["q_norm"][l]), wqu=wqu.astype(BF16), kvn=r1(p["kv_norm"][l]),
        wk=wk.astype(BF16), wvt=wv.T.astype(BF16), wfvt=seg(5).T.astype(BF16),
        onm=r1(p["out_norm_mla"][l]), onf=r1(p["out_norm_fox"][l]), wo=p["w_out"][l].astype(BF16),
        nf=r1(p["norm_ffn"][l]), wrh=wrh, wrl=wrl, br=br,
        wg=p["w_gate"][l].astype(BF16), wu=p["w_up"][l].astype(BF16), wd=p["w_down"][l].astype(BF16),
    )


def _tile(n, pref):
    return pref if n % pref == 0 else n


def _layer(x, mod8, lw, past, cfg):
    B, T, D = x.shape
    tm = cfg["tm"]
    aug_consts = (cfg["tril3"], cfg["pq"], cfg["pk"])
    if past is None:
        P = 0
        cinit = jnp.zeros((B, 8, LANES), F32)
    else:
        P = past[0].shape[1]
        augk_past, cinit = _past_forget(past[4].astype(F32), (cfg["tril3_past"], cfg["pq"], cfg["pk"]),
                                        cfg["tm_past"])
    (ckv, kro, krp, kf, vf, lf, qm, qf, kfb, vfb, aq, ak, vft) = _pre_attention(
        x, mod8, lw, cfg["tabs"], aug_consts, cinit, tm)
    new_rows = (ckv.reshape(B, T, KV_LORA), kro.reshape(B, T, ROPE_DIM),
                kf.reshape(B, T, H_FOX, FOX_DIM), vf.reshape(B, T, H_FOX, FOX_DIM), lf.reshape(B, T, H_FOX))

    S = P + T
    if past is None:
        ckv_all, krp_all, kfb_all, ak_all = ckv, krp, kfb, ak
        Sp = S
    else:
        Sp = -(-S // cfg["tk"]) * cfg["tk"]
        HD = H_FOX * FOX_DIM

        def cat(old, new, w):
            a = jnp.concatenate([old.reshape(B, P, w).astype(new.dtype), new.reshape(B, T, w)], axis=1)
            return jnp.pad(a, ((0, 0), (0, Sp - S), (0, 0))).reshape(B * Sp, w)

        ckv_all = cat(past[0], ckv, KV_LORA)
        krp_all = cat(jnp.pad(past[1], ((0, 0), (0, 0), (0, LANES - ROPE_DIM))), krp, LANES)
        kfb_all = cat(past[2], kfb, HD)
        vft = _vt_from_rows(cat(past[3], vfb, HD), B, Sp)
        ak_all = cat(augk_past, ak, HD)
    km, vmt = _kv_up(ckv_all, krp_all, lw["wk"], lw["wvt"], B, Sp, _tile(Sp, 512))
    att = dict(B=B, T=T, S=Sp, tq=cfg["tq"], tk=cfg["tk"], p0=P)
    om = _attention((qm,), (km,), vmt, fox=False, **att)
    of = _attention((qf, aq), (kfb_all, ak_all), vft, fox=True, **att)
    x1, hg = _post_attention(om, of, x, mod8, lw, tm)
    ys, pos = _moe(hg, lw, cfg["tms"], tm)
    x2 = _residual(x1, ys, pos, mod8, tm).reshape(B, T, D)
    return x2, new_rows


def _group_cfg(B, T, P, pq, pk):
    tm = _tile(T, 256)
    tq = _tile(T, 512)
    tk = tq if P == 0 else (512 if P % 512 == 0 else LANES)
    cfg = dict(tm=tm, tq=tq, tk=tk, tms=_tile(B * T, 512), tril3=_tril3(tm), pq=pq, pk=pk,
               tabs=_rope_tables(P, T))
    if P:
        cfg["tm_past"] = _tile(P, 256)
        cfg["tril3_past"] = _tril3(cfg["tm_past"])
    return cfg


def kernel(x_prompt, x_sample, cache_mla_ckv, cache_mla_krope, cache_fox_k, cache_fox_v, cache_fox_logf, c_prompt, c_sample, w_mod, b_mod, norm_attn, w_in, b_forget, q_norm, w_q_up, kv_norm, w_kv_up, out_norm_mla, out_norm_fox, w_out, norm_ffn, w_router_group, b_router_group, w_router_expert, b_router_expert, w_gate, w_up, w_down, final_norm):
    params = dict(norm_attn=norm_attn, w_in=w_in, b_forget=b_forget, q_norm=q_norm, w_q_up=w_q_up,
                  kv_norm=kv_norm, w_kv_up=w_kv_up, out_norm_mla=out_norm_mla, out_norm_fox=out_norm_fox,
                  w_out=w_out, norm_ffn=norm_ffn, w_router_group=w_router_group, b_router_group=b_router_group,
                  w_router_expert=w_router_expert, b_router_expert=b_router_expert,
                  w_gate=w_gate, w_up=w_up, w_down=w_down)
    L = w_mod.shape[0]
    Bp, Tp, D = x_prompt.shape
    Bs, Ts, _ = x_sample.shape
    P = cache_mla_ckv.shape[2]

    R = -(-(Bp + Bs) // 8) * 8
    c_all = jnp.pad(jnp.concatenate([c_prompt, c_sample], axis=0), ((0, R - Bp - Bs), (0, 0)))
    mod = _mod_all(c_all, w_mod, b_mod).reshape(L, R, 6, D)
    mod = jnp.pad(mod, ((0, 0), (0, 0), (0, 2), (0, 0)))

    pq, pk = _aug_placement()
    cfg_p = _group_cfg(Bp, Tp, 0, pq, pk)
    cfg_s = _group_cfg(Bs, Ts, P, pq, pk)

    y_p, y_s = x_prompt, x_sample
    rows_p, rows_s = [], []
    for l in range(L):
        lw = _layer_weights(l, params)
        y_p, r_p = _layer(y_p, mod[l, :Bp], lw, None, cfg_p)
        past = (cache_mla_ckv[l], cache_mla_krope[l], cache_fox_k[l], cache_fox_v[l], cache_fox_logf[l])
        y_s, r_s = _layer(y_s, mod[l, Bp:Bp + Bs], lw, past, cfg_s)
        rows_p.append(r_p)
        rows_s.append(r_s)

    y_prompt = _final_norm(y_p, final_norm, _tile(Bp * Tp, 512))
    y_sample = _final_norm(y_s, final_norm, _tile(Bs * Ts, 512))
    stack = lambda rows, k: jnp.stack([r[k] for r in rows])
    return (y_prompt, y_sample,
            stack(rows_p, 0), stack(rows_p, 1), stack(rows_p, 2), stack(rows_p, 3), stack(rows_p, 4),
            stack(rows_s, 0), stack(rows_s, 1), stack(rows_s, 2), stack(rows_s, 3), stack(rows_s, 4))
```

```python
import functools

import numpy as np
import jax
import jax.numpy as jnp
from jax import lax
from jax.experimental import pallas as pl
from jax.experimental.pallas import tpu as pltpu

CHUNK = 64
CHUNK_SHIFT = 6
EPS = 1e-6
H_MLA = 8
NOPE_DIM = 64
ROPE_DIM = 32
V_DIM = 64
Q_LORA = 384
KV_LORA = 256
ROPE_THETA = 10000.0
MLA_SCALE = (NOPE_DIM + ROPE_DIM) ** -0.5
H_FOX = 8
FOX_DIM = 64
FOX_SCALE = FOX_DIM ** -0.5
N_GROUPS = 4
EXPERTS_PER_GROUP = 8
N_EXPERTS = N_GROUPS * EXPERTS_PER_GROUP
D_EXPERT = 256

LANES = 128
HEAD_PAD = LANES
VT_ROWS = 80
ROUTER_OFF = N_GROUPS
GIDX_LANE = 16
NEG = -1e30
LOG2E = 1.4426950408889634
VMEM_LIMIT = 56 * 1024 * 1024

F32 = jnp.float32
BF16 = jnp.bfloat16


def _cparams(sem):
    return pltpu.CompilerParams(dimension_semantics=sem, vmem_limit_bytes=VMEM_LIMIT)


def _rms(x, g):
    return x * lax.rsqrt(jnp.mean(x * x, axis=-1, keepdims=True) + EPS) * g


def _dot(a, b):
    return jnp.dot(a, b, preferred_element_type=F32)


def _split3(x):
    hi = x.astype(BF16).astype(F32)
    r = x - hi
    mid = r.astype(BF16).astype(F32)
    lo = (r - mid).astype(BF16).astype(F32)
    return hi, mid, lo


def _vt_rows(vt):
    n = vt.shape[1]
    r = lax.broadcasted_iota(jnp.int32, (VT_ROWS - V_DIM, n), 0)
    tail = jnp.where(r == 0, 1.0, 0.0).astype(BF16)
    vb = vt.astype(BF16)
    parts = []
    for h in range(H_MLA):
        parts += [vb[h * V_DIM:(h + 1) * V_DIM], tail]
    return jnp.concatenate(parts, axis=0)


def _nt_dot(a, b):
    return lax.dot_general(a, b, (((1,), (1,)), ((), ())), preferred_element_type=F32)


def _mod_kernel(c_ref, w_ref, b_ref, o_ref):
    o_ref[0] = _dot(c_ref[...].astype(BF16), w_ref[0].astype(BF16)) + b_ref[0]


def _mod_all(c_all, w_mod, b_mod):
    L, D, N6 = w_mod.shape
    R = c_all.shape[0]
    TN = N6 // 4
    return pl.pallas_call(
        _mod_kernel,
        out_shape=jax.ShapeDtypeStruct((L, R, N6), F32),
        grid=(L, N6 // TN),
        in_specs=[pl.BlockSpec((R, D), lambda l, j: (0, 0)),
                  pl.BlockSpec((1, D, TN), lambda l, j: (l, 0, j)),
                  pl.BlockSpec((1, 1, TN), lambda l, j: (l, 0, j))],
        out_specs=pl.BlockSpec((1, R, TN), lambda l, j: (l, 0, j)),
        compiler_params=_cparams(("arbitrary", "arbitrary")),
        name="adaln_mod",
    )(c_all, w_mod, b_mod.reshape(L, 1, N6))


def _fox_cum_aug(logf, carry_row, tril3_ref, pq_ref, pk_ref):
    hi, mid, lo = _split3(logf)
    stacked = jnp.concatenate([hi.astype(BF16), mid.astype(BF16), lo.astype(BF16)], axis=0)
    cum = _dot(tril3_ref[...], stacked) + carry_row
    lane = lax.broadcasted_iota(jnp.int32, cum.shape, 1)
    cum = jnp.where(lane < H_FOX, cum, 0.0)
    fh, fm, fl = _split3(cum * LOG2E)
    a3 = (fh + pltpu.roll(fm, H_FOX, axis=1) + pltpu.roll(fl, 2 * H_FOX, axis=1)
          + jnp.where(lane == 3 * H_FOX, 1.0, 0.0)).astype(BF16)
    augq = _dot(a3, pq_ref[...]).astype(BF16)
    augk = _dot(a3, pk_ref[...]).astype(BF16)
    return cum, augq, augk


def _aug_placement():
    pq = np.zeros((LANES, H_FOX * FOX_DIM), np.float32)
    pk = np.zeros((LANES, H_FOX * FOX_DIM), np.float32)
    for h in range(H_FOX):
        base = h * FOX_DIM
        for j in range(3):
            pq[3 * H_FOX, base + j] = 1.0
            pq[j * H_FOX + h, base + 3 + j] = 1.0
            pk[j * H_FOX + h, base + j] = -1.0
            pk[3 * H_FOX, base + 3 + j] = 1.0
    return jnp.asarray(pq, BF16), jnp.asarray(pk, BF16)


def _tril3(tm):
    t = np.tril(np.ones((tm, tm), np.float32))
    return jnp.asarray(np.concatenate([t, t, t], axis=1), BF16)


def _rope_lanes(x, c, s1, s2):
    return x * c + pltpu.roll(x, LANES - ROPE_DIM // 2, axis=1) * s1 + pltpu.roll(x, ROPE_DIM // 2, axis=1) * s2


def _pre_kernel(x_ref, mod_ref, na_ref, wq_ref, wkv_ref, wkr_ref, wfq_ref, wfk_ref, wfv_ref, wfl_ref, bfl_ref,
                qn_ref, wqu_ref, kvn_ref, wfvt_ref, cq_ref, s1q_ref, s2q_ref, ck_ref, s1k_ref, s2k_ref,
                tril3_ref, pq_ref, pk_ref, cinit_ref,
                ckv_ref, kro_ref, krp_ref, kf_ref, vf_ref, lf_ref, qm_ref, qf_ref, kfb_ref, vfb_ref,
                aq_ref, ak_ref, vft_ref, carry_ref):
    @pl.when(pl.program_id(1) == 0)
    def _():
        carry_ref[...] = cinit_ref[0]

    x = x_ref[...]
    shift = mod_ref[0, 0:1, :]
    scale = mod_ref[0, 1:2, :]
    h = _rms(x, na_ref[...]) * (1.0 + scale) + shift
    hb = h.astype(BF16)

    qn = _rms(_dot(hb, wq_ref[...]), qn_ref[...]).astype(BF16)
    q = _dot(qn, wqu_ref[...])
    cq, s1q, s2q = cq_ref[...], s1q_ref[...], s2q_ref[...]
    for hh in range(H_MLA):
        sl = slice(hh * HEAD_PAD, (hh + 1) * HEAD_PAD)
        qm_ref[:, sl] = _rope_lanes(q[:, sl], cq, s1q, s2q).astype(BF16)

    ckv_ref[...] = _rms(_dot(hb, wkv_ref[...]), kvn_ref[...])
    kr = _rope_lanes(_dot(hb, wkr_ref[...]), ck_ref[...], s1k_ref[...], s2k_ref[...])
    kro_ref[...] = kr[:, :ROPE_DIM]
    krp_ref[...] = kr.astype(BF16)

    qf_ref[...] = (_dot(hb, wfq_ref[...]) * (FOX_SCALE * LOG2E)).astype(BF16)
    kf = _dot(hb, wfk_ref[...])
    kf_ref[...] = kf
    kfb_ref[...] = kf.astype(BF16)
    vf = _dot(hb, wfv_ref[...])
    vf_ref[...] = vf
    vfb_ref[...] = vf.astype(BF16)
    vft_ref[...] = _vt_rows(_nt_dot(wfvt_ref[...], hb))
    z = _dot(hb, wfl_ref[...]) + bfl_ref[...]
    logf = jnp.minimum(z, 0.0) - jnp.log1p(jnp.exp(-jnp.abs(z)))
    lane = lax.broadcasted_iota(jnp.int32, logf.shape, 1)
    logf = jnp.where(lane < H_FOX, logf, 0.0)
    lf_ref[...] = logf[:, :H_FOX]
    cum, augq, augk = _fox_cum_aug(logf, carry_ref[0:1, :], tril3_ref, pq_ref, pk_ref)
    tm = cum.shape[0]
    carry_ref[...] = jnp.broadcast_to(cum[tm - 1:tm, :], carry_ref.shape)
    aq_ref[...] = augq
    ak_ref[...] = augk


def _pre_attention(x, mod8, lw, tabs, consts, cinit, tm):
    B, T, D = x.shape
    nt = T // tm
    N = B * T
    xf = x.reshape(N, D)
    row = lambda w: pl.BlockSpec((tm, w), lambda b, i: (b * nt + i, 0))
    full = lambda a: pl.BlockSpec(a.shape, lambda b, i: (0,) * a.ndim)
    tab = pl.BlockSpec((tm, LANES), lambda b, i: (i, 0))
    weights = [lw["na"], lw["wq"], lw["wkv"], lw["wkr"], lw["wfq"], lw["wfk"], lw["wfv"], lw["wfl"], lw["bfl"],
               lw["qn"], lw["wqu"], lw["kvn"], lw["wfvt"]]
    HD = H_FOX * FOX_DIM
    outs = [(KV_LORA, F32), (ROPE_DIM, F32), (LANES, BF16), (HD, F32), (HD, F32), (H_FOX, F32),
            (H_MLA * HEAD_PAD, BF16), (HD, BF16), (HD, BF16), (HD, BF16), (HD, BF16), (HD, BF16)]
    VR = H_FOX * VT_ROWS
    return pl.pallas_call(
        _pre_kernel,
        out_shape=[jax.ShapeDtypeStruct((N, w), dt) for w, dt in outs] + [jax.ShapeDtypeStruct((B * VR, T), BF16)],
        grid=(B, nt),
        in_specs=([row(D), pl.BlockSpec((1, 8, D), lambda b, i: (b, 0, 0))] + [full(w) for w in weights]
                  + [tab] * 6 + [full(c) for c in consts]
                  + [pl.BlockSpec((1, 8, LANES), lambda b, i: (b, 0, 0))]),
        out_specs=[row(w) for w, _ in outs] + [pl.BlockSpec((VR, tm), lambda b, i: (b, i))],
        scratch_shapes=[pltpu.VMEM((8, LANES), F32)],
        compiler_params=_cparams(("arbitrary", "arbitrary")),
        name="pre_attention",
    )(xf, mod8, *weights, *tabs, *consts, cinit)


def _past_kernel(lf_ref, tril3_ref, pq_ref, pk_ref, ak_ref, last_ref, carry_ref):
    @pl.when(pl.program_id(1) == 0)
    def _():
        carry_ref[...] = jnp.zeros_like(carry_ref)

    cum, _, augk = _fox_cum_aug(lf_ref[...], carry_ref[0:1, :], tril3_ref, pq_ref, pk_ref)
    tm = cum.shape[0]
    carry_ref[...] = jnp.broadcast_to(cum[tm - 1:tm, :], carry_ref.shape)
    ak_ref[...] = augk
    last_ref[0] = carry_ref[...]


def _past_forget(logf_past, consts, tm):
    B, P, H = logf_past.shape
    nt = P // tm
    lf = jnp.pad(logf_past.reshape(B * P, H), ((0, 0), (0, LANES - H)))
    full = lambda a: pl.BlockSpec(a.shape, lambda b, i: (0,) * a.ndim)
    return pl.pallas_call(
        _past_kernel,
        out_shape=[jax.ShapeDtypeStruct((B * P, H_FOX * FOX_DIM), BF16),
                   jax.ShapeDtypeStruct((B, 8, LANES), F32)],
        grid=(B, nt),
        in_specs=[pl.BlockSpec((tm, LANES), lambda b, i: (b * nt + i, 0))] + [full(c) for c in consts],
        out_specs=[pl.BlockSpec((tm, H_FOX * FOX_DIM), lambda b, i: (b * nt + i, 0)),
                   pl.BlockSpec((1, 8, LANES), lambda b, i: (b, 0, 0))],
        scratch_shapes=[pltpu.VMEM((8, LANES), F32)],
        compiler_params=_cparams(("arbitrary", "arbitrary")),
        name="past_forget",
    )(lf, *consts)


def _kvup_kernel(ckv_ref, krp_ref, wk_ref, wvt_ref, k_ref, vt_ref):
    cb = ckv_ref[...].astype(BF16)
    k_ref[...] = _dot(jnp.concatenate([cb, krp_ref[...]], axis=1), wk_ref[...]).astype(BF16)
    vt_ref[...] = _vt_rows(_nt_dot(wvt_ref[...], cb))


def _kv_up(ckv, krp, wk, wvt, B, S, tm):
    nt = S // tm
    row = lambda w: pl.BlockSpec((tm, w), lambda b, i: (b * nt + i, 0))
    full = lambda a: pl.BlockSpec(a.shape, lambda b, i: (0,) * a.ndim)
    VR = H_MLA * VT_ROWS
    return pl.pallas_call(
        _kvup_kernel,
        out_shape=[jax.ShapeDtypeStruct((B * S, H_MLA * HEAD_PAD), BF16),
                   jax.ShapeDtypeStruct((B * VR, S), BF16)],
        grid=(B, nt),
        in_specs=[row(KV_LORA), row(LANES), full(wk), full(wvt)],
        out_specs=[row(H_MLA * HEAD_PAD), pl.BlockSpec((VR, tm), lambda b, i: (b, i))],
        compiler_params=_cparams(("arbitrary", "arbitrary")),
        name="mla_kv_up",
    )(ckv, krp, wk, wvt)


def _attn_kernel(*refs, fox, tq, tk, p0):
    if fox:
        q_ref, aq_ref, k_ref, ak_ref, vt_ref, o_ref, qs_ref, sa_ref, sb_ref, m_ref, acc_ref = refs
    else:
        q_ref, k_ref, vt_ref, o_ref, sa_ref, sb_ref, m_ref, acc_ref = refs
    q0 = p0 + pl.program_id(2) * tq
    nfull = q0 // tk

    if fox:
        lane = lax.broadcasted_iota(jnp.int32, (tq, LANES), 1)
        q = q_ref[...]
        aq = aq_ref[...]
        zero = jnp.zeros_like(q)
        for j in range(2):
            own = (lane >= j * FOX_DIM) & (lane < (j + 1) * FOX_DIM)
            qs_ref[j] = jnp.concatenate([jnp.where(own, q, zero), jnp.where(own, aq, zero)], axis=1)

    m_ref[...] = jnp.full(m_ref.shape, NEG, F32)
    acc_ref[...] = jnp.zeros(acc_ref.shape, F32)

    def scores(kt, s_ref):
        ks = pl.multiple_of(kt * tk, tk)
        if fox:
            kj = jnp.concatenate([k_ref[pl.ds(ks, tk), :], ak_ref[pl.ds(ks, tk), :]], axis=1)
        for j in range(2):
            if fox:
                qj = qs_ref[j]
            else:
                qj = q_ref[:, j * HEAD_PAD:(j + 1) * HEAD_PAD]
                kj = k_ref[pl.ds(ks, tk), j * HEAD_PAD:(j + 1) * HEAD_PAD]
            s_ref[j] = lax.dot_general(kj, qj, (((1,), (1,)), ((), ())), preferred_element_type=F32)

    def softmax_pv(kt, s_ref, masked):
        ks = pl.multiple_of(kt * tk, tk)
        if masked:
            kpos = ks + lax.broadcasted_iota(jnp.int32, (tk, tq), 0)
            qpos = q0 + lax.broadcasted_iota(jnp.int32, (tk, tq), 1)
            if fox:
                ok = kpos <= qpos
            else:
                ok = jnp.right_shift(kpos, CHUNK_SHIFT) <= jnp.right_shift(qpos, CHUNK_SHIFT)
        for j in range(2):
            s = s_ref[j]
            if masked:
                s = jnp.where(ok, s, NEG)
            m_prev = m_ref[j]
            m_new = jnp.maximum(m_prev, jnp.max(s, axis=0, keepdims=True))
            p = jnp.exp2(s - m_new)
            alpha = jnp.exp2(m_prev - m_new)
            vt = vt_ref[j * VT_ROWS:(j + 1) * VT_ROWS, pl.ds(ks, tk)]
            acc_ref[j] = alpha * acc_ref[j] + _dot(vt, p.astype(BF16))
            m_ref[j] = m_new

    scores(0, sa_ref)

    def pair_body(c, carry):
        kt = 2 * c
        scores(kt + 1, sb_ref)
        softmax_pv(kt, sa_ref, False)
        scores(kt + 2, sa_ref)
        softmax_pv(kt + 1, sb_ref, False)
        return carry

    lax.fori_loop(0, nfull // 2, pair_body, 0)
    odd = (nfull % 2) == 1

    @pl.when(odd)
    def _():
        scores(nfull, sb_ref)
        softmax_pv(nfull - 1, sa_ref, False)
        softmax_pv(nfull, sb_ref, True)

    @pl.when(jnp.logical_not(odd))
    def _():
        softmax_pv(nfull, sa_ref, True)

    outs = []
    for j in range(2):
        acc = acc_ref[j]
        outs.append(acc[:V_DIM] * (1.0 / acc[V_DIM:V_DIM + 1]))
    o_ref[...] = jnp.concatenate(outs, axis=0).T


def _vt_from_rows(v, B, S):
    vt = v.reshape(B, S, H_MLA, V_DIM).transpose(0, 2, 3, 1)
    vt = jnp.concatenate([vt, jnp.ones((B, H_MLA, 1, S), BF16),
                          jnp.zeros((B, H_MLA, VT_ROWS - V_DIM - 1, S), BF16)], axis=2)
    return vt.reshape(B * H_MLA * VT_ROWS, S)


def _attention(qs, ks, vt, *, fox, B, T, S, tq, tk, p0):
    H2 = H_MLA // 2
    T_in = T
    if T < LANES:
        qs = tuple(jnp.pad(q.reshape(B, T, -1), ((0, 0), (0, LANES - T), (0, 0))).reshape(B * LANES, -1)
                   for q in qs)
        T = tq = LANES
    assert tk % tq == 0 and p0 % tk == 0 and S % tk == 0 and S >= p0 + T, (tq, tk, p0, S, T)
    assert tq == tk or T == tq, (tq, tk, T)
    nq = T // tq
    qw = LANES if fox else 2 * HEAD_PAD
    qspec = pl.BlockSpec((tq, qw), lambda b, h, i: (b * nq + i, h))
    kspec = pl.BlockSpec((S, qw), lambda b, h, i: (b, h))
    vspec = pl.BlockSpec((2 * VT_ROWS, S), lambda b, h, i: (b * H2 + h, 0))
    if fox:
        in_specs = [qspec, qspec, kspec, kspec, vspec]
        args = (qs[0], qs[1], ks[0], ks[1], vt)
        scratch = [pltpu.VMEM((2, tq, 2 * LANES), BF16)]
    else:
        in_specs = [qspec, kspec, vspec]
        args = (qs[0], ks[0], vt)
        scratch = []
    scratch += [pltpu.VMEM((2, tk, tq), F32), pltpu.VMEM((2, tk, tq), F32),
                pltpu.VMEM((2, 1, tq), F32), pltpu.VMEM((2, VT_ROWS, tq), F32)]
    out = pl.pallas_call(
        functools.partial(_attn_kernel, fox=fox, tq=tq, tk=tk, p0=p0),
        out_shape=jax.ShapeDtypeStruct((B * T, H_MLA * V_DIM), F32),
        grid=(B, H2, nq),
        in_specs=in_specs,
        out_specs=pl.BlockSpec((tq, LANES), lambda b, h, i: (b * nq + i, h)),
        scratch_shapes=scratch,
        compiler_params=_cparams(("arbitrary", "arbitrary", "arbitrary")),
        name="fox_attention" if fox else "mla_attention",
    )(*args)
    if T_in != T:
        out = out.reshape(B, T, -1)[:, :T_in].reshape(B * T_in, -1)
    return out


def _post_kernel(om_ref, of_ref, x_ref, mod_ref, onm_ref, onf_ref, wo_ref, nf_ref, wrh_ref, wrl_ref, br_ref,
                 x1_ref, hg_ref):
    o = jnp.concatenate([_rms(om_ref[...], onm_ref[...]), _rms(of_ref[...], onf_ref[...])], axis=1)
    mix = _dot(o.astype(BF16), wo_ref[...])
    gate1 = mod_ref[0, 2:3, :]
    shift2 = mod_ref[0, 3:4, :]
    scale2 = mod_ref[0, 4:5, :]
    x1 = x_ref[...] + gate1 * mix
    x1_ref[...] = x1
    h2 = _rms(x1, nf_ref[...]) * (1.0 + scale2) + shift2
    hh = h2.astype(BF16)
    hg_ref[:, :h2.shape[1]] = hh.astype(F32)
    hl = (h2 - hh.astype(F32)).astype(BF16)
    wrh = wrh_ref[...]
    logits = _dot(hh, wrh) + _dot(hl, wrh) + _dot(hh, wrl_ref[...]) + br_ref[...]

    lane = lax.broadcasted_iota(jnp.int32, logits.shape, 1).astype(F32)
    big = float(LANES)
    gmask = lane < N_GROUPS
    gl = jnp.where(gmask, logits, NEG)
    gmax = jnp.max(gl, axis=1, keepdims=True)
    gsum = jnp.sum(jnp.where(gmask, jnp.exp(gl - gmax), 0.0), axis=1, keepdims=True)
    g_p = 1.0 / gsum
    g_idx = jnp.min(jnp.where(gl == gmax, lane, big), axis=1, keepdims=True)
    lo = ROUTER_OFF + g_idx * EXPERTS_PER_GROUP
    emask = (lane >= lo) & (lane < lo + EXPERTS_PER_GROUP)
    el = jnp.where(emask, logits, NEG)
    emax = jnp.max(el, axis=1, keepdims=True)
    ee = jnp.where(emask, jnp.exp(el - emax), 0.0)
    ep = ee / jnp.sum(ee, axis=1, keepdims=True)
    p1 = jnp.max(ep, axis=1, keepdims=True)
    i1 = jnp.min(jnp.where(emask & (ep == p1), lane, big), axis=1, keepdims=True)
    rest = emask & (lane != i1)
    ep2 = jnp.where(rest, ep, -1.0)
    p2 = jnp.max(ep2, axis=1, keepdims=True)
    i2 = jnp.min(jnp.where(rest & (ep2 == p2), lane, big), axis=1, keepdims=True)
    den = p1 + p2
    gate = (jnp.where(lane == i1, g_p * (p1 / den), 0.0) + jnp.where(lane == i2, g_p * (p2 / den), 0.0))
    g8 = jnp.zeros_like(gate)
    for g in range(N_GROUPS):
        g8 = g8 + pltpu.roll(gate, LANES - (ROUTER_OFF + g * EXPERTS_PER_GROUP), axis=1)
    d = h2.shape[1]
    hg_ref[:, d:] = jnp.where(lane < EXPERTS_PER_GROUP, g8, jnp.where(lane == GIDX_LANE, g_idx, 0.0))


def _post_attention(om, of, x, mod8, lw, tm):
    B, T, D = x.shape
    nt = T // tm
    N = B * T
    row = lambda w: pl.BlockSpec((tm, w), lambda b, i: (b * nt + i, 0))
    full = lambda a: pl.BlockSpec(a.shape, lambda b, i: (0,) * a.ndim)
    weights = [lw["onm"], lw["onf"], lw["wo"], lw["nf"], lw["wrh"], lw["wrl"], lw["br"]]
    HV = H_MLA * V_DIM
    return pl.pallas_call(
        _post_kernel,
        out_shape=[jax.ShapeDtypeStruct((N, D), F32), jax.ShapeDtypeStruct((N, D + LANES), F32)],
        grid=(B, nt),
        in_specs=[row(HV), row(HV), row(D), pl.BlockSpec((1, 8, D), lambda b, i: (b, 0, 0))]
                 + [full(w) for w in weights],
        out_specs=[row(D), row(D + LANES)],
        compiler_params=_cparams(("arbitrary", "arbitrary")),
        name="post_attention",
    )(om, of, x.reshape(N, D), mod8, *weights)


def _row_copies(n, make, start):
    def body(r, c):
        cp = make(r)
        cp.start() if start else cp.wait()
        return c
    lax.fori_loop(0, n, body, 0, unroll=8)


def _dispatch_kernel(pos_ref, h_ref, hs_in, hs_out, sem, *, tm):
    del hs_in

    def copy(r, start):
        dst = pos_ref[0, 0, r] if start else 0
        return pltpu.make_async_copy(h_ref.at[pl.ds(r, 1), :], hs_out.at[pl.ds(dst, 1), :], sem.at[0])

    _row_copies(tm, lambda r: copy(r, True), True)
    _row_copies(tm, lambda r: copy(r, False), False)


def _dispatch(hg, pos3, R, tm):
    N, W = hg.shape
    return pl.pallas_call(
        functools.partial(_dispatch_kernel, tm=tm),
        out_shape=jax.ShapeDtypeStruct((R, W), F32),
        grid=(N // tm,),
        in_specs=[pl.BlockSpec((1, 1, tm), lambda t: (t, 0, 0), memory_space=pltpu.SMEM),
                  pl.BlockSpec((tm, W), lambda t: (t, 0)), pl.BlockSpec(memory_space=pl.ANY)],
        out_specs=pl.BlockSpec(memory_space=pl.ANY),
        scratch_shapes=[pltpu.SemaphoreType.DMA((1,))],
        input_output_aliases={2: 0},
        compiler_params=_cparams(("arbitrary",)),
        name="moe_dispatch",
    )(pos3, hg, jnp.zeros((R, W), F32))


def _moe_kernel(tg_ref, hs_ref, wg_ref, wu_ref, wd_ref, ys_ref, *, d):
    del tg_ref
    rows = hs_ref[...]
    h = rows[:, :d].astype(BF16)
    g8 = rows[:, d:]
    lane = lax.broadcasted_iota(jnp.int32, g8.shape, 1)
    acc = jnp.zeros((rows.shape[0], d), F32)
    for j in range(EXPERTS_PER_GROUP):
        a = _dot(h, wg_ref[j])
        u = _dot(h, wu_ref[j])
        gcol = jnp.sum(jnp.where(lane == j, g8, 0.0), axis=1, keepdims=True)
        act = (a * (1.0 / (1.0 + jnp.exp(-a)))) * u * gcol
        acc = acc + _dot(act.astype(BF16), wd_ref[j])
    ys_ref[...] = acc


def _moe(hg, lw, tm, tmd):
    N, W = hg.shape
    d = W - LANES
    R = N + N_GROUPS * tm
    nt = R // tm
    grp = hg[:, d + GIDX_LANE].astype(jnp.int32)
    onehot = (grp[:, None] == jnp.arange(N_GROUPS, dtype=jnp.int32)[None, :]).astype(jnp.int32)
    counts = jnp.sum(onehot, axis=0)
    padded = ((counts + tm - 1) // tm) * tm
    ends = jnp.cumsum(padded)
    starts = ends - padded
    rank = jnp.sum((jnp.cumsum(onehot, axis=0) - 1) * onehot, axis=1)
    pos = jnp.sum(onehot * starts[None, :], axis=1) + rank
    tile_first = jnp.arange(nt, dtype=jnp.int32) * tm
    tile_group = jnp.minimum(jnp.sum((tile_first[:, None] >= ends[None, :]).astype(jnp.int32), axis=1),
                             N_GROUPS - 1)

    hs = _dispatch(hg, pos.reshape(N // tmd, 1, tmd), R, tmd)
    wspec = lambda a: pl.BlockSpec((EXPERTS_PER_GROUP,) + a.shape[1:], lambda t, tg: (tg[t], 0, 0))
    grid_spec = pltpu.PrefetchScalarGridSpec(
        num_scalar_prefetch=1,
        grid=(nt,),
        in_specs=[pl.BlockSpec((tm, W), lambda t, tg: (t, 0)),
                  wspec(lw["wg"]), wspec(lw["wu"]), wspec(lw["wd"])],
        out_specs=pl.BlockSpec((tm, d), lambda t, tg: (t, 0)),
    )
    ys = pl.pallas_call(
        functools.partial(_moe_kernel, d=d),
        out_shape=jax.ShapeDtypeStruct((R, d), F32),
        grid_spec=grid_spec,
        compiler_params=_cparams(("arbitrary",)),
        name="moe_experts",
    )(tile_group, hs, lw["wg"], lw["wu"], lw["wd"])
    return ys, pos


def _residual_kernel(pos_ref, nxt_ref, x_ref, mod_ref, ys_hbm, o_ref, ybuf, sem, *, tm):
    nt = pl.num_programs(1)
    step = pl.program_id(0) * nt + pl.program_id(1)
    total = pl.num_programs(0) * nt
    slot = step % 2

    def copy(idx_ref, r, s, start):
        src = idx_ref[0, 0, r] if start else 0
        return pltpu.make_async_copy(ys_hbm.at[pl.ds(src, 1), :], ybuf.at[s, pl.ds(r, 1), :], sem.at[s])

    @pl.when(step == 0)
    def _():
        _row_copies(tm, lambda r: copy(pos_ref, r, 0, True), True)

    @pl.when(step + 1 < total)
    def _():
        _row_copies(tm, lambda r: copy(nxt_ref, r, 1 - slot, True), True)

    _row_copies(tm, lambda r: copy(pos_ref, r, slot, False), False)
    o_ref[...] = x_ref[...] + mod_ref[0, 5:6, :] * ybuf[slot]


def _residual(x1, ys, pos, mod8, tm):
    B = mod8.shape[0]
    N, D = x1.shape
    nt = N // B // tm
    ns = N // tm
    pos3 = pos.reshape(ns, 1, tm)
    row = pl.BlockSpec((tm, D), lambda b, i: (b * nt + i, 0))
    idx = lambda f: pl.BlockSpec((1, 1, tm), f, memory_space=pltpu.SMEM)
    return pl.pallas_call(
        functools.partial(_residual_kernel, tm=tm),
        out_shape=jax.ShapeDtypeStruct((N, D), F32),
        grid=(B, nt),
        in_specs=[idx(lambda b, i: (b * nt + i, 0, 0)),
                  idx(lambda b, i: (jnp.minimum(b * nt + i + 1, ns - 1), 0, 0)),
                  row, pl.BlockSpec((1, 8, D), lambda b, i: (b, 0, 0)), pl.BlockSpec(memory_space=pl.ANY)],
        out_specs=row,
        scratch_shapes=[pltpu.VMEM((2, tm, D), F32), pltpu.SemaphoreType.DMA((2,))],
        compiler_params=_cparams(("arbitrary", "arbitrary")),
        name="ffn_residual",
    )(pos3, pos3, x1, mod8, ys)


def _norm_kernel(x_ref, g_ref, o_ref):
    o_ref[...] = _rms(x_ref[...], g_ref[...])


def _final_norm(x, g, tm):
    B, T, D = x.shape
    N = B * T
    return pl.pallas_call(
        _norm_kernel,
        out_shape=jax.ShapeDtypeStruct((N, D), F32),
        grid=(N // tm,),
        in_specs=[pl.BlockSpec((tm, D), lambda i: (i, 0)), pl.BlockSpec((1, D), lambda i: (0, 0))],
        out_specs=pl.BlockSpec((tm, D), lambda i: (i, 0)),
        compiler_params=_cparams(("arbitrary",)),
        name="final_norm",
    )(x.reshape(N, D), g.reshape(1, D)).reshape(B, T, D)


def _rope_tables(p0, T):
    half = ROPE_DIM // 2
    inv = ROPE_THETA ** (-jnp.arange(half, dtype=F32) / half)
    ang = (p0 + jnp.arange(T, dtype=jnp.int32)).astype(F32)[:, None] * inv[None, :]
    cos, sin = jnp.cos(ang), jnp.sin(ang)
    z = lambda w: jnp.zeros((T, w), F32)
    one = jnp.ones((T, NOPE_DIM), F32)
    tail = z(HEAD_PAD - NOPE_DIM - ROPE_DIM)
    qs = MLA_SCALE * LOG2E
    cq = jnp.concatenate([one, cos, cos, tail], axis=1) * qs
    s1q = jnp.concatenate([z(NOPE_DIM), -sin, z(half), tail], axis=1) * qs
    s2q = jnp.concatenate([z(NOPE_DIM), z(half), sin, tail], axis=1) * qs
    kt = z(LANES - ROPE_DIM)
    ck = jnp.concatenate([cos, cos, kt], axis=1)
    s1k = jnp.concatenate([-sin, z(half), kt], axis=1)
    s2k = jnp.concatenate([z(half), sin, kt], axis=1)
    return cq, s1q, s2q, ck, s1k, s2k


def _layer_weights(l, p):
    D = p["w_in"].shape[1]
    w_in = p["w_in"][l]
    o = np.cumsum([0, Q_LORA, KV_LORA, ROPE_DIM, H_FOX * FOX_DIM, H_FOX * FOX_DIM, H_FOX * FOX_DIM, H_FOX])
    seg = lambda i: w_in[:, o[i]:o[i + 1]]
    padl = lambda w: jnp.pad(w, ((0, 0), (0, LANES - w.shape[1])))
    r1 = lambda v: v.reshape(1, -1).astype(F32)
    wqu = p["w_q_up"][l].reshape(Q_LORA, H_MLA, NOPE_DIM + ROPE_DIM)
    wqu = jnp.pad(wqu, ((0, 0), (0, 0), (0, HEAD_PAD - NOPE_DIM - ROPE_DIM))).reshape(Q_LORA, H_MLA * HEAD_PAD)
    wkv = p["w_kv_up"][l].reshape(KV_LORA, H_MLA, NOPE_DIM + V_DIM)
    wk_c = jnp.pad(wkv[:, :, :NOPE_DIM], ((0, 0), (0, 0), (0, HEAD_PAD - NOPE_DIM)))
    place = np.zeros((LANES, H_MLA, HEAD_PAD), np.float32)
    for j in range(ROPE_DIM):
        place[j, :, NOPE_DIM + j] = 1.0
    wk = jnp.concatenate([wk_c.reshape(KV_LORA, H_MLA * HEAD_PAD),
                          jnp.asarray(place.reshape(LANES, H_MLA * HEAD_PAD))], axis=0)
    wv = wkv[:, :, NOPE_DIM:].reshape(KV_LORA, H_MLA * V_DIM)
    wr = padl(jnp.concatenate([p["w_router_group"][l], p["w_router_expert"][l]], axis=1))
    wrh = wr.astype(BF16)
    wrl = (wr - wrh.astype(F32)).astype(BF16)
    br = padl(jnp.concatenate([p["b_router_group"][l], p["b_router_expert"][l]]).reshape(1, -1)).astype(F32)
    return dict(
        na=r1(p["norm_attn"][l]), wq=seg(0).astype(BF16), wkv=seg(1).astype(BF16), wkr=padl(seg(2)).astype(BF16),
        wfq=seg(3).astype(BF16), wfk=seg(4).astype(BF16), wfv=seg(5).astype(BF16), wfl=padl(seg(6)).astype(BF16),
        bfl=padl(r1(p["b_forget"][l])), qn=r1(p["q_norm"][l]), wqu=wqu.astype(BF16), kvn=r1(p["kv_norm"][l]),
        wk=wk.astype(BF16), wvt=wv.T.astype(BF16), wfvt=seg(5).T.astype(BF16),
        onm=r1(p["out_norm_mla"][l]), onf=r1(p["out_norm_fox"][l]), wo=p["w_out"][l].astype(BF16),
        nf=r1(p["norm_ffn"][l]), wrh=wrh, wrl=wrl, br=br,
        wg=p["w_gate"][l].astype(BF16), wu=p["w_up"][l].astype(BF16), wd=p["w_down"][l].astype(BF16),
    )


def _tile(n, pref):
    return pref if n % pref == 0 else n


def _layer(x, mod8, lw, past, cfg):
    B, T, D = x.shape
    tm = cfg["tm"]
    aug_consts = (cfg["tril3"], cfg["pq"], cfg["pk"])
    if past is None:
        P = 0
        cinit = jnp.zeros((B, 8, LANES), F32)
    else:
        P = past[0].shape[1]
        augk_past, cinit = _past_forget(past[4].astype(F32), (cfg["tril3_past"], cfg["pq"], cfg["pk"]),
                                        cfg["tm_past"])
    (ckv, kro, krp, kf, vf, lf, qm, qf, kfb, vfb, aq, ak, vft) = _pre_attention(
        x, mod8, lw, cfg["tabs"], aug_consts, cinit, tm)
    new_rows = (ckv.reshape(B, T, KV_LORA), kro.reshape(B, T, ROPE_DIM),
                kf.reshape(B, T, H_FOX, FOX_DIM), vf.reshape(B, T, H_FOX, FOX_DIM), lf.reshape(B, T, H_FOX))

    S = P + T
    if past is None:
        ckv_all, krp_all, kfb_all, ak_all = ckv, krp, kfb, ak
        Sp = S
    else:
        Sp = -(-S // cfg["tk"]) * cfg["tk"]
        HD = H_FOX * FOX_DIM

        def cat(old, new, w):
            a = jnp.concatenate([old.reshape(B, P, w).astype(new.dtype), new.reshape(B, T, w)], axis=1)
            return jnp.pad(a, ((0, 0), (0, Sp - S), (0, 0))).reshape(B * Sp, w)

        ckv_all = cat(past[0], ckv, KV_LORA)
        krp_all = cat(jnp.pad(past[1], ((0, 0), (0, 0), (0, LANES - ROPE_DIM))), krp, LANES)
        kfb_all = cat(past[2], kfb, HD)
        vft = _vt_from_rows(cat(past[3], vfb, HD), B, Sp)
        ak_all = cat(augk_past, ak, HD)
    km, vmt = _kv_up(ckv_all, krp_all, lw["wk"], lw["wvt"], B, Sp, _tile(Sp, 512))
    att = dict(B=B, T=T, S=Sp, tq=cfg["tq"], tk=cfg["tk"], p0=P)
    om = _attention((qm,), (km,), vmt, fox=False, **att)
    of = _attention((qf, aq), (kfb_all, ak_all), vft, fox=True, **att)
    x1, hg = _post_attention(om, of, x, mod8, lw, tm)
    ys, pos = _moe(hg, lw, cfg["tms"], cfg["tms"])
    x2 = _residual(x1, ys, pos, mod8, tm).reshape(B, T, D)
    return x2, new_rows


def _group_cfg(B, T, P, pq, pk):
    tm = _tile(T, 256)
    tq = _tile(T, 512)
    tk = tq if P == 0 else (512 if P % 512 == 0 else LANES)
    cfg = dict(tm=tm, tq=tq, tk=tk, tms=_tile(B * T, 512), tril3=_tril3(tm), pq=pq, pk=pk,
               tabs=_rope_tables(P, T))
    if P:
        cfg["tm_past"] = _tile(P, 256)
        cfg["tril3_past"] = _tril3(cfg["tm_past"])
    return cfg


def kernel(x_prompt, x_sample, cache_mla_ckv, cache_mla_krope, cache_fox_k, cache_fox_v, cache_fox_logf, c_prompt, c_sample, w_mod, b_mod, norm_attn, w_in, b_forget, q_norm, w_q_up, kv_norm, w_kv_up, out_norm_mla, out_norm_fox, w_out, norm_ffn, w_router_group, b_router_group, w_router_expert, b_router_expert, w_gate, w_up, w_down, final_norm):
    params = dict(norm_attn=norm_attn, w_in=w_in, b_forget=b_forget, q_norm=q_norm, w_q_up=w_q_up,
                  kv_norm=kv_norm, w_kv_up=w_kv_up, out_norm_mla=out_norm_mla, out_norm_fox=out_norm_fox,
                  w_out=w_out, norm_ffn=norm_ffn, w_router_group=w_router_group, b_router_group=b_router_group,
                  w_router_expert=w_router_expert, b_router_expert=b_router_expert,
                  w_gate=w_gate, w_up=w_up, w_down=w_down)
    L = w_mod.shape[0]
    Bp, Tp, D = x_prompt.shape
    Bs, Ts, _ = x_sample.shape
    P = cache_mla_ckv.shape[2]

    R = -(-(Bp + Bs) // 8) * 8
    c_all = jnp.pad(jnp.concatenate([c_prompt, c_sample], axis=0), ((0, R - Bp - Bs), (0, 0)))
    mod = _mod_all(c_all, w_mod, b_mod).reshape(L, R, 6, D)
    mod = jnp.pad(mod, ((0, 0), (0, 0), (0, 2), (0, 0)))

    pq, pk = _aug_placement()
    cfg_p = _group_cfg(Bp, Tp, 0, pq, pk)
    cfg_s = _group_cfg(Bs, Ts, P, pq, pk)

    y_p, y_s = x_prompt, x_sample
    rows_p, rows_s = [], []
    for l in range(L):
        lw = _layer_weights(l, params)
        y_p, r_p = _layer(y_p, mod[l, :Bp], lw, None, cfg_p)
        past = (cache_mla_ckv[l], cache_mla_krope[l], cache_fox_k[l], cache_fox_v[l], cache_fox_logf[l])
        y_s, r_s = _layer(y_s, mod[l, Bp:Bp + Bs], lw, past, cfg_s)
        rows_p.append(r_p)
        rows_s.append(r_s)

    y_prompt = _final_norm(y_p, final_norm, _tile(Bp * Tp, 512))
    y_sample = _final_norm(y_s, final_norm, _tile(Bs * Ts, 512))
    stack = lambda rows, k: jnp.stack([r[k] for r in rows])
    return (y_prompt, y_sample,
            stack(rows_p, 0), stack(rows_p, 1), stack(rows_p, 2), stack(rows_p, 3), stack(rows_p, 4),
            stack(rows_s, 0), stack(rows_s, 1), stack(rows_s, 2), stack(rows_s, 3), stack(rows_s, 4))
```

```python
import functools

import numpy as np
import jax
import jax.numpy as jnp
from jax import lax
from jax.experimental import pallas as pl
from jax.experimental.pallas import tpu as pltpu

CHUNK = 64
CHUNK_SHIFT = 6
EPS = 1e-6
H_MLA = 8
NOPE_DIM = 64
ROPE_DIM = 32
V_DIM = 64
Q_LORA = 384
KV_LORA = 256
ROPE_THETA = 10000.0
MLA_SCALE = (NOPE_DIM + ROPE_DIM) ** -0.5
H_FOX = 8
FOX_DIM = 64
FOX_SCALE = FOX_DIM ** -0.5
N_GROUPS = 4
EXPERTS_PER_GROUP = 8
N_EXPERTS = N_GROUPS * EXPERTS_PER_GROUP
D_EXPERT = 256

LANES = 128
HEAD_PAD = LANES
VT_ROWS = 80
ROUTER_OFF = N_GROUPS
GIDX_LANE = 16
NEG = -1e30
LOG2E = 1.4426950408889634
VMEM_LIMIT = 56 * 1024 * 1024

F32 = jnp.float32
BF16 = jnp.bfloat16


def _cparams(sem):
    return pltpu.CompilerParams(dimension_semantics=sem, vmem_limit_bytes=VMEM_LIMIT)


def _rms(x, g):
    return x * lax.rsqrt(jnp.mean(x * x, axis=-1, keepdims=True) + EPS) * g


def _dot(a, b):
    return jnp.dot(a, b, preferred_element_type=F32)


def _split3(x):
    hi = x.astype(BF16).astype(F32)
    r = x - hi
    mid = r.astype(BF16).astype(F32)
    lo = (r - mid).astype(BF16).astype(F32)
    return hi, mid, lo


def _vt_rows(vt):
    n = vt.shape[1]
    r = lax.broadcasted_iota(jnp.int32, (VT_ROWS - V_DIM, n), 0)
    tail = jnp.where(r == 0, 1.0, 0.0).astype(BF16)
    vb = vt.astype(BF16)
    parts = []
    for h in range(H_MLA):
        parts += [vb[h * V_DIM:(h + 1) * V_DIM], tail]
    return jnp.concatenate(parts, axis=0)


def _nt_dot(a, b):
    return lax.dot_general(a, b, (((1,), (1,)), ((), ())), preferred_element_type=F32)


def _mod_kernel(c_ref, w_ref, b_ref, o_ref):
    o_ref[0] = _dot(c_ref[...].astype(BF16), w_ref[0].astype(BF16)) + b_ref[0]


def _mod_all(c_all, w_mod, b_mod):
    L, D, N6 = w_mod.shape
    R = c_all.shape[0]
    TN = N6 // 4
    return pl.pallas_call(
        _mod_kernel,
        out_shape=jax.ShapeDtypeStruct((L, R, N6), F32),
        grid=(L, N6 // TN),
        in_specs=[pl.BlockSpec((R, D), lambda l, j: (0, 0)),
                  pl.BlockSpec((1, D, TN), lambda l, j: (l, 0, j)),
                  pl.BlockSpec((1, 1, TN), lambda l, j: (l, 0, j))],
        out_specs=pl.BlockSpec((1, R, TN), lambda l, j: (l, 0, j)),
        compiler_params=_cparams(("arbitrary", "arbitrary")),
        name="adaln_mod",
    )(c_all, w_mod, b_mod.reshape(L, 1, N6))


def _fox_cum_aug(logf, carry_row, tril3_ref, pq_ref, pk_ref):
    hi, mid, lo = _split3(logf)
    stacked = jnp.concatenate([hi.astype(BF16), mid.astype(BF16), lo.astype(BF16)], axis=0)
    cum = _dot(tril3_ref[...], stacked) + carry_row
    lane = lax.broadcasted_iota(jnp.int32, cum.shape, 1)
    cum = jnp.where(lane < H_FOX, cum, 0.0)
    fh, fm, fl = _split3(cum * LOG2E)
    a3 = (fh + pltpu.roll(fm, H_FOX, axis=1) + pltpu.roll(fl, 2 * H_FOX, axis=1)
          + jnp.where(lane == 3 * H_FOX, 1.0, 0.0)).astype(BF16)
    augq = _dot(a3, pq_ref[...]).astype(BF16)
    augk = _dot(a3, pk_ref[...]).astype(BF16)
    return cum, augq, augk


def _aug_placement():
    pq = np.zeros((LANES, H_FOX * FOX_DIM), np.float32)
    pk = np.zeros((LANES, H_FOX * FOX_DIM), np.float32)
    for h in range(H_FOX):
        base = h * FOX_DIM
        for j in range(3):
            pq[3 * H_FOX, base + j] = 1.0
            pq[j * H_FOX + h, base + 3 + j] = 1.0
            pk[j * H_FOX + h, base + j] = -1.0
            pk[3 * H_FOX, base + 3 + j] = 1.0
    return jnp.asarray(pq, BF16), jnp.asarray(pk, BF16)


def _tril3(tm):
    t = np.tril(np.ones((tm, tm), np.float32))
    return jnp.asarray(np.concatenate([t, t, t], axis=1), BF16)


def _rope_lanes(x, c, s1, s2):
    return x * c + pltpu.roll(x, LANES - ROPE_DIM // 2, axis=1) * s1 + pltpu.roll(x, ROPE_DIM // 2, axis=1) * s2


def _pre_kernel(x_ref, mod_ref, na_ref, wq_ref, wkv_ref, wkr_ref, wfq_ref, wfk_ref, wfv_ref, wfl_ref, bfl_ref,
                qn_ref, wqu_ref, kvn_ref, wfvt_ref, cq_ref, s1q_ref, s2q_ref, ck_ref, s1k_ref, s2k_ref,
                tril3_ref, pq_ref, pk_ref, cinit_ref,
                ckv_ref, kro_ref, krp_ref, kf_ref, vf_ref, lf_ref, qm_ref, qf_ref, kfb_ref, vfb_ref,
                aq_ref, ak_ref, vft_ref, carry_ref):
    @pl.when(pl.program_id(1) == 0)
    def _():
        carry_ref[...] = cinit_ref[0]

    x = x_ref[...]
    shift = mod_ref[0, 0:1, :]
    scale = mod_ref[0, 1:2, :]
    h = _rms(x, na_ref[...]) * (1.0 + scale) + shift
    hb = h.astype(BF16)

    qn = _rms(_dot(hb, wq_ref[...]), qn_ref[...]).astype(BF16)
    q = _dot(qn, wqu_ref[...])
    cq, s1q, s2q = cq_ref[...], s1q_ref[...], s2q_ref[...]
    for hh in range(H_MLA):
        sl = slice(hh * HEAD_PAD, (hh + 1) * HEAD_PAD)
        qm_ref[:, sl] = _rope_lanes(q[:, sl], cq, s1q, s2q).astype(BF16)

    ckv_ref[...] = _rms(_dot(hb, wkv_ref[...]), kvn_ref[...])
    kr = _rope_lanes(_dot(hb, wkr_ref[...]), ck_ref[...], s1k_ref[...], s2k_ref[...])
    kro_ref[...] = kr[:, :ROPE_DIM]
    krp_ref[...] = kr.astype(BF16)

    qf_ref[...] = (_dot(hb, wfq_ref[...]) * (FOX_SCALE * LOG2E)).astype(BF16)
    kf = _dot(hb, wfk_ref[...])
    kf_ref[...] = kf
    kfb_ref[...] = kf.astype(BF16)
    vf = _dot(hb, wfv_ref[...])
    vf_ref[...] = vf
    vfb_ref[...] = vf.astype(BF16)
    vft_ref[...] = _vt_rows(_nt_dot(wfvt_ref[...], hb))
    z = _dot(hb, wfl_ref[...]) + bfl_ref[...]
    logf = jnp.minimum(z, 0.0) - jnp.log1p(jnp.exp(-jnp.abs(z)))
    lane = lax.broadcasted_iota(jnp.int32, logf.shape, 1)
    logf = jnp.where(lane < H_FOX, logf, 0.0)
    lf_ref[...] = logf[:, :H_FOX]
    cum, augq, augk = _fox_cum_aug(logf, carry_ref[0:1, :], tril3_ref, pq_ref, pk_ref)
    tm = cum.shape[0]
    carry_ref[...] = jnp.broadcast_to(cum[tm - 1:tm, :], carry_ref.shape)
    aq_ref[...] = augq
    ak_ref[...] = augk


def _pre_attention(x, mod8, lw, tabs, consts, cinit, tm):
    B, T, D = x.shape
    nt = T // tm
    N = B * T
    xf = x.reshape(N, D)
    row = lambda w: pl.BlockSpec((tm, w), lambda b, i: (b * nt + i, 0))
    full = lambda a: pl.BlockSpec(a.shape, lambda b, i: (0,) * a.ndim)
    tab = pl.BlockSpec((tm, LANES), lambda b, i: (i, 0))
    weights = [lw["na"], lw["wq"], lw["wkv"], lw["wkr"], lw["wfq"], lw["wfk"], lw["wfv"], lw["wfl"], lw["bfl"],
               lw["qn"], lw["wqu"], lw["kvn"], lw["wfvt"]]
    HD = H_FOX * FOX_DIM
    outs = [(KV_LORA, F32), (ROPE_DIM, F32), (LANES, BF16), (HD, F32), (HD, F32), (H_FOX, F32),
            (H_MLA * HEAD_PAD, BF16), (HD, BF16), (HD, BF16), (HD, BF16), (HD, BF16), (HD, BF16)]
    VR = H_FOX * VT_ROWS
    return pl.pallas_call(
        _pre_kernel,
        out_shape=[jax.ShapeDtypeStruct((N, w), dt) for w, dt in outs] + [jax.ShapeDtypeStruct((B * VR, T), BF16)],
        grid=(B, nt),
        in_specs=([row(D), pl.BlockSpec((1, 8, D), lambda b, i: (b, 0, 0))] + [full(w) for w in weights]
                  + [tab] * 6 + [full(c) for c in consts]
                  + [pl.BlockSpec((1, 8, LANES), lambda b, i: (b, 0, 0))]),
        out_specs=[row(w) for w, _ in outs] + [pl.BlockSpec((VR, tm), lambda b, i: (b, i))],
        scratch_shapes=[pltpu.VMEM((8, LANES), F32)],
        compiler_params=_cparams(("arbitrary", "arbitrary")),
        name="pre_attention",
    )(xf, mod8, *weights, *tabs, *consts, cinit)


def _past_kernel(lf_ref, tril3_ref, pq_ref, pk_ref, ak_ref, last_ref, carry_ref):
    @pl.when(pl.program_id(1) == 0)
    def _():
        carry_ref[...] = jnp.zeros_like(carry_ref)

    cum, _, augk = _fox_cum_aug(lf_ref[...], carry_ref[0:1, :], tril3_ref, pq_ref, pk_ref)
    tm = cum.shape[0]
    carry_ref[...] = jnp.broadcast_to(cum[tm - 1:tm, :], carry_ref.shape)
    ak_ref[...] = augk
    last_ref[0] = carry_ref[...]


def _past_forget(logf_past, consts, tm):
    B, P, H = logf_past.shape
    nt = P // tm
    lf = jnp.pad(logf_past.reshape(B * P, H), ((0, 0), (0, LANES - H)))
    full = lambda a: pl.BlockSpec(a.shape, lambda b, i: (0,) * a.ndim)
    return pl.pallas_call(
        _past_kernel,
        out_shape=[jax.ShapeDtypeStruct((B * P, H_FOX * FOX_DIM), BF16),
                   jax.ShapeDtypeStruct((B, 8, LANES), F32)],
        grid=(B, nt),
        in_specs=[pl.BlockSpec((tm, LANES), lambda b, i: (b * nt + i, 0))] + [full(c) for c in consts],
        out_specs=[pl.BlockSpec((tm, H_FOX * FOX_DIM), lambda b, i: (b * nt + i, 0)),
                   pl.BlockSpec((1, 8, LANES), lambda b, i: (b, 0, 0))],
        scratch_shapes=[pltpu.VMEM((8, LANES), F32)],
        compiler_params=_cparams(("arbitrary", "arbitrary")),
        name="past_forget",
    )(lf, *consts)


def _kvup_kernel(ckv_ref, krp_ref, wk_ref, wvt_ref, k_ref, vt_ref):
    cb = ckv_ref[...].astype(BF16)
    k_ref[...] = _dot(jnp.concatenate([cb, krp_ref[...]], axis=1), wk_ref[...]).astype(BF16)
    vt_ref[...] = _vt_rows(_nt_dot(wvt_ref[...], cb))


def _kv_up(ckv, krp, wk, wvt, B, S, tm):
    nt = S // tm
    row = lambda w: pl.BlockSpec((tm, w), lambda b, i: (b * nt + i, 0))
    full = lambda a: pl.BlockSpec(a.shape, lambda b, i: (0,) * a.ndim)
    VR = H_MLA * VT_ROWS
    return pl.pallas_call(
        _kvup_kernel,
        out_shape=[jax.ShapeDtypeStruct((B * S, H_MLA * HEAD_PAD), BF16),
                   jax.ShapeDtypeStruct((B * VR, S), BF16)],
        grid=(B, nt),
        in_specs=[row(KV_LORA), row(LANES), full(wk), full(wvt)],
        out_specs=[row(H_MLA * HEAD_PAD), pl.BlockSpec((VR, tm), lambda b, i: (b, i))],
        compiler_params=_cparams(("arbitrary", "arbitrary")),
        name="mla_kv_up",
    )(ckv, krp, wk, wvt)


def _attn_kernel(*refs, fox, tq, tk, p0, nq):
    if fox:
        q_ref, aq_ref, k_ref, ak_ref, vt_ref, o_ref, qs_ref, sa_ref, sb_ref, m_ref, acc_ref = refs
    else:
        q_ref, k_ref, vt_ref, o_ref, sa_ref, sb_ref, m_ref, acc_ref = refs
    i = pl.program_id(2)
    q0 = p0 + i * tq
    nfull = q0 // tk

    def build_q(iq):
        if not fox:
            return None
        rs = pl.multiple_of(iq * tq, tq)
        lane = lax.broadcasted_iota(jnp.int32, (tq, LANES), 1)
        q = q_ref[pl.ds(rs, tq), :]
        aq = aq_ref[pl.ds(rs, tq), :]
        zero = jnp.zeros_like(q)
        vals = []
        for j in range(2):
            own = (lane >= j * FOX_DIM) & (lane < (j + 1) * FOX_DIM)
            vals.append(jnp.concatenate([jnp.where(own, q, zero), jnp.where(own, aq, zero)], axis=1))
            qs_ref[j] = vals[j]
        return vals

    m_ref[...] = jnp.full(m_ref.shape, NEG, F32)
    acc_ref[...] = jnp.zeros(acc_ref.shape, F32)

    def scores(kt, s_ref, iq, qvals=None):
        ks = pl.multiple_of(kt * tk, tk)
        rs = pl.multiple_of(iq * tq, tq)
        if fox:
            kj = jnp.concatenate([k_ref[pl.ds(ks, tk), :], ak_ref[pl.ds(ks, tk), :]], axis=1)
        for j in range(2):
            if fox:
                qj = qs_ref[j] if qvals is None else qvals[j]
            else:
                qj = q_ref[pl.ds(rs, tq), j * HEAD_PAD:(j + 1) * HEAD_PAD]
                kj = k_ref[pl.ds(ks, tk), j * HEAD_PAD:(j + 1) * HEAD_PAD]
            s_ref[j] = lax.dot_general(kj, qj, (((1,), (1,)), ((), ())), preferred_element_type=F32)

    def softmax_pv(kt, s_ref, masked):
        ks = pl.multiple_of(kt * tk, tk)
        if masked:
            kpos = ks + lax.broadcasted_iota(jnp.int32, (tk, tq), 0)
            qpos = q0 + lax.broadcasted_iota(jnp.int32, (tk, tq), 1)
            if fox:
                ok = kpos <= qpos
            else:
                ok = jnp.right_shift(kpos, CHUNK_SHIFT) <= jnp.right_shift(qpos, CHUNK_SHIFT)
        for j in range(2):
            s = s_ref[j]
            if masked:
                s = jnp.where(ok, s, NEG)
            m_prev = m_ref[j]
            m_new = jnp.maximum(m_prev, jnp.max(s, axis=0, keepdims=True))
            p = jnp.exp2(s - m_new)
            alpha = jnp.exp2(m_prev - m_new)
            vt = vt_ref[j * VT_ROWS:(j + 1) * VT_ROWS, pl.ds(ks, tk)]
            acc_ref[j] = alpha * acc_ref[j] + _dot(vt, p.astype(BF16))
            m_ref[j] = m_new

    def run(x_ref, y_ref, chain):
        def pair_body(c, carry):
            kt = 2 * c
            scores(kt + 1, y_ref, i)
            softmax_pv(kt, x_ref, False)
            scores(kt + 2, x_ref, i)
            softmax_pv(kt + 1, y_ref, False)
            return carry

        lax.fori_loop(0, nfull // 2, pair_body, 0)
        odd = (nfull % 2) == 1

        def tail(masked_ref, free_ref):
            if chain:
                inext = jnp.minimum(i + 1, nq - 1)
                scores(0, free_ref, inext, build_q(inext))
            softmax_pv(nfull, masked_ref, True)

        @pl.when(odd)
        def _():
            scores(nfull, y_ref, i)
            softmax_pv(nfull - 1, x_ref, False)
            tail(y_ref, x_ref)

        @pl.when(jnp.logical_not(odd))
        def _():
            tail(x_ref, y_ref)

    if nq == 1:
        scores(0, sa_ref, i, build_q(i))
        run(sa_ref, sb_ref, False)
    else:
        @pl.when(i == 0)
        def _():
            scores(0, sa_ref, i, build_q(i))

        in_a = (((i + 1) // 2) % 2) == 0

        @pl.when(in_a)
        def _():
            run(sa_ref, sb_ref, True)

        @pl.when(jnp.logical_not(in_a))
        def _():
            run(sb_ref, sa_ref, True)

    outs = []
    for j in range(2):
        acc = acc_ref[j]
        outs.append(acc[:V_DIM] * (1.0 / acc[V_DIM:V_DIM + 1]))
    o_ref[...] = jnp.concatenate(outs, axis=0).T


def _vt_from_rows(v, B, S):
    vt = v.reshape(B, S, H_MLA, V_DIM).transpose(0, 2, 3, 1)
    vt = jnp.concatenate([vt, jnp.ones((B, H_MLA, 1, S), BF16),
                          jnp.zeros((B, H_MLA, VT_ROWS - V_DIM - 1, S), BF16)], axis=2)
    return vt.reshape(B * H_MLA * VT_ROWS, S)


def _attention(qs, ks, vt, *, fox, B, T, S, tq, tk, p0):
    H2 = H_MLA // 2
    T_in = T
    if T < LANES:
        qs = tuple(jnp.pad(q.reshape(B, T, -1), ((0, 0), (0, LANES - T), (0, 0))).reshape(B * LANES, -1)
                   for q in qs)
        T = tq = LANES
    assert tk % tq == 0 and p0 % tk == 0 and S % tk == 0 and S >= p0 + T, (tq, tk, p0, S, T)
    nq = T // tq
    assert nq == 1 or (tq == tk and p0 == 0), (tq, tk, T, p0)
    qw = LANES if fox else 2 * HEAD_PAD
    qspec = pl.BlockSpec((T, qw), lambda b, h, i: (b, h))
    kspec = pl.BlockSpec((S, qw), lambda b, h, i: (b, h))
    vspec = pl.BlockSpec((2 * VT_ROWS, S), lambda b, h, i: (b * H2 + h, 0))
    if fox:
        in_specs = [qspec, qspec, kspec, kspec, vspec]
        args = (qs[0], qs[1], ks[0], ks[1], vt)
        scratch = [pltpu.VMEM((2, tq, 2 * LANES), BF16)]
    else:
        in_specs = [qspec, kspec, vspec]
        args = (qs[0], ks[0], vt)
        scratch = []
    scratch += [pltpu.VMEM((2, tk, tq), F32), pltpu.VMEM((2, tk, tq), F32),
                pltpu.VMEM((2, 1, tq), F32), pltpu.VMEM((2, VT_ROWS, tq), F32)]
    out = pl.pallas_call(
        functools.partial(_attn_kernel, fox=fox, tq=tq, tk=tk, p0=p0, nq=nq),
        out_shape=jax.ShapeDtypeStruct((B * T, H_MLA * V_DIM), F32),
        grid=(B, H2, nq),
        in_specs=in_specs,
        out_specs=pl.BlockSpec((tq, LANES), lambda b, h, i: (b * nq + i, h)),
        scratch_shapes=scratch,
        compiler_params=_cparams(("arbitrary", "arbitrary", "arbitrary")),
        name="fox_attention" if fox else "mla_attention",
    )(*args)
    if T_in != T:
        out = out.reshape(B, T, -1)[:, :T_in].reshape(B * T_in, -1)
    return out


def _post_kernel(om_ref, of_ref, x_ref, mod_ref, onm_ref, onf_ref, wo_ref, nf_ref, wrh_ref, wrl_ref, br_ref,
                 x1_ref, hg_ref):
    o = jnp.concatenate([_rms(om_ref[...], onm_ref[...]), _rms(of_ref[...], onf_ref[...])], axis=1)
    mix = _dot(o.astype(BF16), wo_ref[...])
    gate1 = mod_ref[0, 2:3, :]
    shift2 = mod_ref[0, 3:4, :]
    scale2 = mod_ref[0, 4:5, :]
    x1 = x_ref[...] + gate1 * mix
    x1_ref[...] = x1
    h2 = _rms(x1, nf_ref[...]) * (1.0 + scale2) + shift2
    hh = h2.astype(BF16)
    hg_ref[:, :h2.shape[1]] = hh.astype(F32)
    hl = (h2 - hh.astype(F32)).astype(BF16)
    wrh = wrh_ref[...]
    logits = _dot(hh, wrh) + _dot(hl, wrh) + _dot(hh, wrl_ref[...]) + br_ref[...]

    lane = lax.broadcasted_iota(jnp.int32, logits.shape, 1).astype(F32)
    big = float(LANES)
    gmask = lane < N_GROUPS
    gl = jnp.where(gmask, logits, NEG)
    gmax = jnp.max(gl, axis=1, keepdims=True)
    gsum = jnp.sum(jnp.where(gmask, jnp.exp(gl - gmax), 0.0), axis=1, keepdims=True)
    g_p = 1.0 / gsum
    g_idx = jnp.min(jnp.where(gl == gmax, lane, big), axis=1, keepdims=True)
    lo = ROUTER_OFF + g_idx * EXPERTS_PER_GROUP
    emask = (lane >= lo) & (lane < lo + EXPERTS_PER_GROUP)
    el = jnp.where(emask, logits, NEG)
    emax = jnp.max(el, axis=1, keepdims=True)
    ee = jnp.where(emask, jnp.exp(el - emax), 0.0)
    ep = ee / jnp.sum(ee, axis=1, keepdims=True)
    p1 = jnp.max(ep, axis=1, keepdims=True)
    i1 = jnp.min(jnp.where(emask & (ep == p1), lane, big), axis=1, keepdims=True)
    rest = emask & (lane != i1)
    ep2 = jnp.where(rest, ep, -1.0)
    p2 = jnp.max(ep2, axis=1, keepdims=True)
    i2 = jnp.min(jnp.where(rest & (ep2 == p2), lane, big), axis=1, keepdims=True)
    den = p1 + p2
    gate = (jnp.where(lane == i1, g_p * (p1 / den), 0.0) + jnp.where(lane == i2, g_p * (p2 / den), 0.0))
    g8 = jnp.zeros_like(gate)
    for g in range(N_GROUPS):
        g8 = g8 + pltpu.roll(gate, LANES - (ROUTER_OFF + g * EXPERTS_PER_GROUP), axis=1)
    d = h2.shape[1]
    hg_ref[:, d:] = jnp.where(lane < EXPERTS_PER_GROUP, g8, jnp.where(lane == GIDX_LANE, g_idx, 0.0))


def _post_attention(om, of, x, mod8, lw, tm):
    B, T, D = x.shape
    nt = T // tm
    N = B * T
    row = lambda w: pl.BlockSpec((tm, w), lambda b, i: (b * nt + i, 0))
    full = lambda a: pl.BlockSpec(a.shape, lambda b, i: (0,) * a.ndim)
    weights = [lw["onm"], lw["onf"], lw["wo"], lw["nf"], lw["wrh"], lw["wrl"], lw["br"]]
    HV = H_MLA * V_DIM
    return pl.pallas_call(
        _post_kernel,
        out_shape=[jax.ShapeDtypeStruct((N, D), F32), jax.ShapeDtypeStruct((N, D + LANES), F32)],
        grid=(B, nt),
        in_specs=[row(HV), row(HV), row(D), pl.BlockSpec((1, 8, D), lambda b, i: (b, 0, 0))]
                 + [full(w) for w in weights],
        out_specs=[row(D), row(D + LANES)],
        compiler_params=_cparams(("arbitrary", "arbitrary")),
        name="post_attention",
    )(om, of, x.reshape(N, D), mod8, *weights)


def _row_copies(n, make, start):
    def body(r, c):
        cp = make(r)
        cp.start() if start else cp.wait()
        return c
    lax.fori_loop(0, n, body, 0, unroll=8)


def _dispatch_kernel(pos_ref, h_ref, hs_in, hs_out, sem, *, tm):
    del hs_in

    def copy(r, start):
        dst = pos_ref[0, 0, r] if start else 0
        return pltpu.make_async_copy(h_ref.at[pl.ds(r, 1), :], hs_out.at[pl.ds(dst, 1), :], sem.at[0])

    _row_copies(tm, lambda r: copy(r, True), True)
    _row_copies(tm, lambda r: copy(r, False), False)


def _dispatch(hg, pos3, R, tm):
    N, W = hg.shape
    return pl.pallas_call(
        functools.partial(_dispatch_kernel, tm=tm),
        out_shape=jax.ShapeDtypeStruct((R, W), F32),
        grid=(N // tm,),
        in_specs=[pl.BlockSpec((1, 1, tm), lambda t: (t, 0, 0), memory_space=pltpu.SMEM),
                  pl.BlockSpec((tm, W), lambda t: (t, 0)), pl.BlockSpec(memory_space=pl.ANY)],
        out_specs=pl.BlockSpec(memory_space=pl.ANY),
        scratch_shapes=[pltpu.SemaphoreType.DMA((1,))],
        input_output_aliases={2: 0},
        compiler_params=_cparams(("arbitrary",)),
        name="moe_dispatch",
    )(pos3, hg, jnp.zeros((R, W), F32))


def _moe_kernel(tg_ref, hs_ref, wg_ref, wu_ref, wd_ref, ys_ref, *, d):
    del tg_ref
    rows = hs_ref[...]
    h = rows[:, :d].astype(BF16)
    g8 = rows[:, d:]
    lane = lax.broadcasted_iota(jnp.int32, g8.shape, 1)
    acc = jnp.zeros((rows.shape[0], d), F32)
    for j in range(EXPERTS_PER_GROUP):
        a = _dot(h, wg_ref[j])
        u = _dot(h, wu_ref[j])
        gcol = jnp.sum(jnp.where(lane == j, g8, 0.0), axis=1, keepdims=True)
        act = (a * (1.0 / (1.0 + jnp.exp(-a)))) * u * gcol
        acc = acc + _dot(act.astype(BF16), wd_ref[j])
    ys_ref[...] = acc


def _moe(hg, lw, tm, tmd):
    N, W = hg.shape
    d = W - LANES
    R = N + N_GROUPS * tm
    nt = R // tm
    grp = hg[:, d + GIDX_LANE].astype(jnp.int32)
    onehot = (grp[:, None] == jnp.arange(N_GROUPS, dtype=jnp.int32)[None, :]).astype(jnp.int32)
    counts = jnp.sum(onehot, axis=0)
    padded = ((counts + tm - 1) // tm) * tm
    ends = jnp.cumsum(padded)
    starts = ends - padded
    rank = jnp.sum((jnp.cumsum(onehot, axis=0) - 1) * onehot, axis=1)
    pos = jnp.sum(onehot * starts[None, :], axis=1) + rank
    tile_first = jnp.arange(nt, dtype=jnp.int32) * tm
    tile_group = jnp.minimum(jnp.sum((tile_first[:, None] >= ends[None, :]).astype(jnp.int32), axis=1),
                             N_GROUPS - 1)

    hs = _dispatch(hg, pos.reshape(N // tmd, 1, tmd), R, tmd)
    wspec = lambda a: pl.BlockSpec((EXPERTS_PER_GROUP,) + a.shape[1:], lambda t, tg: (tg[t], 0, 0))
    grid_spec = pltpu.PrefetchScalarGridSpec(
        num_scalar_prefetch=1,
        grid=(nt,),
        in_specs=[pl.BlockSpec((tm, W), lambda t, tg: (t, 0)),
                  wspec(lw["wg"]), wspec(lw["wu"]), wspec(lw["wd"])],
        out_specs=pl.BlockSpec((tm, d), lambda t, tg: (t, 0)),
    )
    ys = pl.pallas_call(
        functools.partial(_moe_kernel, d=d),
        out_shape=jax.ShapeDtypeStruct((R, d), F32),
        grid_spec=grid_spec,
        compiler_params=_cparams(("arbitrary",)),
        name="moe_experts",
    )(tile_group, hs, lw["wg"], lw["wu"], lw["wd"])
    return ys, pos


def _residual_kernel(pos_ref, nxt_ref, x_ref, mod_ref, ys_hbm, o_ref, ybuf, sem, *, tm):
    nt = pl.num_programs(1)
    step = pl.program_id(0) * nt + pl.program_id(1)
    total = pl.num_programs(0) * nt
    slot = step % 2

    def copy(idx_ref, r, s, start):
        src = idx_ref[0, 0, r] if start else 0
        return pltpu.make_async_copy(ys_hbm.at[pl.ds(src, 1), :], ybuf.at[s, pl.ds(r, 1), :], sem.at[s])

    @pl.when(step == 0)
    def _():
        _row_copies(tm, lambda r: copy(pos_ref, r, 0, True), True)

    @pl.when(step + 1 < total)
    def _():
        _row_copies(tm, lambda r: copy(nxt_ref, r, 1 - slot, True), True)

    _row_copies(tm, lambda r: copy(pos_ref, r, slot, False), False)
    o_ref[...] = x_ref[...] + mod_ref[0, 5:6, :] * ybuf[slot]


def _residual(x1, ys, pos, mod8, tm):
    B = mod8.shape[0]
    N, D = x1.shape
    nt = N // B // tm
    ns = N // tm
    pos3 = pos.reshape(ns, 1, tm)
    row = pl.BlockSpec((tm, D), lambda b, i: (b * nt + i, 0))
    idx = lambda f: pl.BlockSpec((1, 1, tm), f, memory_space=pltpu.SMEM)
    return pl.pallas_call(
        functools.partial(_residual_kernel, tm=tm),
        out_shape=jax.ShapeDtypeStruct((N, D), F32),
        grid=(B, nt),
        in_specs=[idx(lambda b, i: (b * nt + i, 0, 0)),
                  idx(lambda b, i: (jnp.minimum(b * nt + i + 1, ns - 1), 0, 0)),
                  row, pl.BlockSpec((1, 8, D), lambda b, i: (b, 0, 0)), pl.BlockSpec(memory_space=pl.ANY)],
        out_specs=row,
        scratch_shapes=[pltpu.VMEM((2, tm, D), F32), pltpu.SemaphoreType.DMA((2,))],
        compiler_params=_cparams(("arbitrary", "arbitrary")),
        name="ffn_residual",
    )(pos3, pos3, x1, mod8, ys)


def _norm_kernel(x_ref, g_ref, o_ref):
    o_ref[...] = _rms(x_ref[...], g_ref[...])


def _final_norm(x, g, tm):
    B, T, D = x.shape
    N = B * T
    return pl.pallas_call(
        _norm_kernel,
        out_shape=jax.ShapeDtypeStruct((N, D), F32),
        grid=(N // tm,),
        in_specs=[pl.BlockSpec((tm, D), lambda i: (i, 0)), pl.BlockSpec((1, D), lambda i: (0, 0))],
        out_specs=pl.BlockSpec((tm, D), lambda i: (i, 0)),
        compiler_params=_cparams(("arbitrary",)),
        name="final_norm",
    )(x.reshape(N, D), g.reshape(1, D)).reshape(B, T, D)


def _rope_tables(p0, T):
    half = ROPE_DIM // 2
    inv = ROPE_THETA ** (-jnp.arange(half, dtype=F32) / half)
    ang = (p0 + jnp.arange(T, dtype=jnp.int32)).astype(F32)[:, None] * inv[None, :]
    cos, sin = jnp.cos(ang), jnp.sin(ang)
    z = lambda w: jnp.zeros((T, w), F32)
    one = jnp.ones((T, NOPE_DIM), F32)
    tail = z(HEAD_PAD - NOPE_DIM - ROPE_DIM)
    qs = MLA_SCALE * LOG2E
    cq = jnp.concatenate([one, cos, cos, tail], axis=1) * qs
    s1q = jnp.concatenate([z(NOPE_DIM), -sin, z(half), tail], axis=1) * qs
    s2q = jnp.concatenate([z(NOPE_DIM), z(half), sin, tail], axis=1) * qs
    kt = z(LANES - ROPE_DIM)
    ck = jnp.concatenate([cos, cos, kt], axis=1)
    s1k = jnp.concatenate([-sin, z(half), kt], axis=1)
    s2k = jnp.concatenate([z(half), sin, kt], axis=1)
    return cq, s1q, s2q, ck, s1k, s2k


def _layer_weights(l, p):
    D = p["w_in"].shape[1]
    w_in = p["w_in"][l]
    o = np.cumsum([0, Q_LORA, KV_LORA, ROPE_DIM, H_FOX * FOX_DIM, H_FOX * FOX_DIM, H_FOX * FOX_DIM, H_FOX])
    seg = lambda i: w_in[:, o[i]:o[i + 1]]
    padl = lambda w: jnp.pad(w, ((0, 0), (0, LANES - w.shape[1])))
    r1 = lambda v: v.reshape(1, -1).astype(F32)
    wqu = p["w_q_up"][l].reshape(Q_LORA, H_MLA, NOPE_DIM + ROPE_DIM)
    wqu = jnp.pad(wqu, ((0, 0), (0, 0), (0, HEAD_PAD - NOPE_DIM - ROPE_DIM))).reshape(Q_LORA, H_MLA * HEAD_PAD)
    wkv = p["w_kv_up"][l].reshape(KV_LORA, H_MLA, NOPE_DIM + V_DIM)
    wk_c = jnp.pad(wkv[:, :, :NOPE_DIM], ((0, 0), (0, 0), (0, HEAD_PAD - NOPE_DIM)))
    place = np.zeros((LANES, H_MLA, HEAD_PAD), np.float32)
    for j in range(ROPE_DIM):
        place[j, :, NOPE_DIM + j] = 1.0
    wk = jnp.concatenate([wk_c.reshape(KV_LORA, H_MLA * HEAD_PAD),
                          jnp.asarray(place.reshape(LANES, H_MLA * HEAD_PAD))], axis=0)
    wv = wkv[:, :, NOPE_DIM:].reshape(KV_LORA, H_MLA * V_DIM)
    wr = padl(jnp.concatenate([p["w_router_group"][l], p["w_router_expert"][l]], axis=1))
    wrh = wr.astype(BF16)
    wrl = (wr - wrh.astype(F32)).astype(BF16)
    br = padl(jnp.concatenate([p["b_router_group"][l], p["b_router_expert"][l]]).reshape(1, -1)).astype(F32)
    return dict(
        na=r1(p["norm_attn"][l]), wq=seg(0).astype(BF16), wkv=seg(1).astype(BF16), wkr=padl(seg(2)).astype(BF16),
        wfq=seg(3).astype(BF16), wfk=seg(4).astype(BF16), wfv=seg(5).astype(BF16), wfl=padl(seg(6)).astype(BF16),
        bfl=padl(r1(p["b_forget"][l])), qn=r1(p["q_norm"][l]), wqu=wqu.astype(BF16), kvn=r1(p["kv_norm"][l]),
        wk=wk.astype(BF16), wvt=wv.T.astype(BF16), wfvt=seg(5).T.astype(BF16),
        onm=r1(p["out_norm_mla"][l]), onf=r1(p["out_norm_fox"][l]), wo=p["w_out"][l].astype(BF16),
        nf=r1(p["norm_ffn"][l]), wrh=wrh, wrl=wrl, br=br,
        wg=p["w_gate"][l].astype(BF16), wu=p["w_up"][l].astype(BF16), wd=p["w_down"][l].astype(BF16),
    )


def _tile(n, pref):
    return pref if n % pref == 0 else n


def _layer(x, mod8, lw, past, cfg):
    B, T, D = x.shape
    tm = cfg["tm"]
    aug_consts = (cfg["tril3"], cfg["pq"], cfg["pk"])
    if past is None:
        P = 0
        cinit = jnp.zeros((B, 8, LANES), F32)
    else:
        P = past[0].shape[1]
        augk_past, cinit = _past_forget(past[4].astype(F32), (cfg["tril3_past"], cfg["pq"], cfg["pk"]),
                                        cfg["tm_past"])
    (ckv, kro, krp, kf, vf, lf, qm, qf, kfb, vfb, aq, ak, vft) = _pre_attention(
        x, mod8, lw, cfg["tabs"], aug_consts, cinit, tm)
    new_rows = (ckv.reshape(B, T, KV_LORA), kro.reshape(B, T, ROPE_DIM),
                kf.reshape(B, T, H_FOX, FOX_DIM), vf.reshape(B, T, H_FOX, FOX_DIM), lf.reshape(B, T, H_FOX))

    S = P + T
    if past is None:
        ckv_all, krp_all, kfb_all, ak_all = ckv, krp, kfb, ak
        Sp = S
    else:
        Sp = -(-S // cfg["tk"]) * cfg["tk"]
        HD = H_FOX * FOX_DIM

        def cat(old, new, w):
            a = jnp.concatenate([old.reshape(B, P, w).astype(new.dtype), new.reshape(B, T, w)], axis=1)
            return jnp.pad(a, ((0, 0), (0, Sp - S), (0, 0))).reshape(B * Sp, w)

        ckv_all = cat(past[0], ckv, KV_LORA)
        krp_all = cat(jnp.pad(past[1], ((0, 0), (0, 0), (0, LANES - ROPE_DIM))), krp, LANES)
        kfb_all = cat(past[2], kfb, HD)
        vft = _vt_from_rows(cat(past[3], vfb, HD), B, Sp)
        ak_all = cat(augk_past, ak, HD)
    km, vmt = _kv_up(ckv_all, krp_all, lw["wk"], lw["wvt"], B, Sp, _tile(Sp, 512))
    att = dict(B=B, T=T, S=Sp, tq=cfg["tq"], tk=cfg["tk"], p0=P)
    om = _attention((qm,), (km,), vmt, fox=False, **att)
    of = _attention((qf, aq), (kfb_all, ak_all), vft, fox=True, **att)
    x1, hg = _post_attention(om, of, x, mod8, lw, tm)
    ys, pos = _moe(hg, lw, cfg["tms"], cfg["tms"])
    x2 = _residual(x1, ys, pos, mod8, tm).reshape(B, T, D)
    return x2, new_rows


def _group_cfg(B, T, P, pq, pk):
    tm = _tile(T, 256)
    tq = _tile(T, 512)
    tk = tq if P == 0 else (512 if P % 512 == 0 else LANES)
    cfg = dict(tm=tm, tq=tq, tk=tk, tms=_tile(B * T, 512), tril3=_tril3(tm), pq=pq, pk=pk,
               tabs=_rope_tables(P, T))
    if P:
        cfg["tm_past"] = _tile(P, 256)
        cfg["tril3_past"] = _tril3(cfg["tm_past"])
    return cfg


def kernel(x_prompt, x_sample, cache_mla_ckv, cache_mla_krope, cache_fox_k, cache_fox_v, cache_fox_logf, c_prompt, c_sample, w_mod, b_mod, norm_attn, w_in, b_forget, q_norm, w_q_up, kv_norm, w_kv_up, out_norm_mla, out_norm_fox, w_out, norm_ffn, w_router_group, b_router_group, w_router_expert, b_router_expert, w_gate, w_up, w_down, final_norm):
    params = dict(norm_attn=norm_attn, w_in=w_in, b_forget=b_forget, q_norm=q_norm, w_q_up=w_q_up,
                  kv_norm=kv_norm, w_kv_up=w_kv_up, out_norm_mla=out_norm_mla, out_norm_fox=out_norm_fox,
                  w_out=w_out, norm_ffn=norm_ffn, w_router_group=w_router_group, b_router_group=b_router_group,
                  w_router_expert=w_router_expert, b_router_expert=b_router_expert,
                  w_gate=w_gate, w_up=w_up, w_down=w_down)
    L = w_mod.shape[0]
    Bp, Tp, D = x_prompt.shape
    Bs, Ts, _ = x_sample.shape
    P = cache_mla_ckv.shape[2]

    R = -(-(Bp + Bs) // 8) * 8
    c_all = jnp.pad(jnp.concatenate([c_prompt, c_sample], axis=0), ((0, R - Bp - Bs), (0, 0)))
    mod = _mod_all(c_all, w_mod, b_mod).reshape(L, R, 6, D)
    mod = jnp.pad(mod, ((0, 0), (0, 0), (0, 2), (0, 0)))

    pq, pk = _aug_placement()
    cfg_p = _group_cfg(Bp, Tp, 0, pq, pk)
    cfg_s = _group_cfg(Bs, Ts, P, pq, pk)

    y_p, y_s = x_prompt, x_sample
    rows_p, rows_s = [], []
    for l in range(L):
        lw = _layer_weights(l, params)
        y_p, r_p = _layer(y_p, mod[l, :Bp], lw, None, cfg_p)
        past = (cache_mla_ckv[l], cache_mla_krope[l], cache_fox_k[l], cache_fox_v[l], cache_fox_logf[l])
        y_s, r_s = _layer(y_s, mod[l, Bp:Bp + Bs], lw, past, cfg_s)
        rows_p.append(r_p)
        rows_s.append(r_s)

    y_prompt = _final_norm(y_p, final_norm, _tile(Bp * Tp, 512))
    y_sample = _final_norm(y_s, final_norm, _tile(Bs * Ts, 512))
    stack = lambda rows, k: jnp.stack([r[k] for r in rows])
    return (y_prompt, y_sample,
            stack(rows_p, 0), stack(rows_p, 1), stack(rows_p, 2), stack(rows_p, 3), stack(rows_p, 4),
            stack(rows_s, 0), stack(rows_s, 1), stack(rows_s, 2), stack(rows_s, 3), stack(rows_s, 4))
```

```python
import functools

import numpy as np
import jax
import jax.numpy as jnp
from jax import lax
from jax.experimental import pallas as pl
from jax.experimental.pallas import tpu as pltpu

CHUNK = 64
CHUNK_SHIFT = 6
EPS = 1e-6
H_MLA = 8
NOPE_DIM = 64
ROPE_DIM = 32
V_DIM = 64
Q_LORA = 384
KV_LORA = 256
ROPE_THETA = 10000.0
MLA_SCALE = (NOPE_DIM + ROPE_DIM) ** -0.5
H_FOX = 8
FOX_DIM = 64
FOX_SCALE = FOX_DIM ** -0.5
N_GROUPS = 4
EXPERTS_PER_GROUP = 8
N_EXPERTS = N_GROUPS * EXPERTS_PER_GROUP
D_EXPERT = 256

LANES = 128
HEAD_PAD = LANES
VT_ROWS = 80
ROUTER_OFF = N_GROUPS
GIDX_LANE = 16
NEG = -1e30
LOG2E = 1.4426950408889634
VMEM_LIMIT = 56 * 1024 * 1024

F32 = jnp.float32
BF16 = jnp.bfloat16


def _cparams(sem):
    return pltpu.CompilerParams(dimension_semantics=sem, vmem_limit_bytes=VMEM_LIMIT)


def _rms(x, g):
    return x * lax.rsqrt(jnp.mean(x * x, axis=-1, keepdims=True) + EPS) * g


def _dot(a, b):
    return jnp.dot(a, b, preferred_element_type=F32)


def _split3(x):
    hi = x.astype(BF16).astype(F32)
    r = x - hi
    mid = r.astype(BF16).astype(F32)
    lo = (r - mid).astype(BF16).astype(F32)
    return hi, mid, lo


def _vt_rows(vt):
    n = vt.shape[1]
    r = lax.broadcasted_iota(jnp.int32, (VT_ROWS - V_DIM, n), 0)
    tail = jnp.where(r == 0, 1.0, 0.0).astype(BF16)
    vb = vt.astype(BF16)
    parts = []
    for h in range(H_MLA):
        parts += [vb[h * V_DIM:(h + 1) * V_DIM], tail]
    return jnp.concatenate(parts, axis=0)


def _nt_dot(a, b):
    return lax.dot_general(a, b, (((1,), (1,)), ((), ())), preferred_element_type=F32)


def _mod_kernel(c_ref, w_ref, b_ref, o_ref):
    o_ref[0] = _dot(c_ref[...].astype(BF16), w_ref[0].astype(BF16)) + b_ref[0]


def _mod_all(c_all, w_mod, b_mod):
    L, D, N6 = w_mod.shape
    R = c_all.shape[0]
    TN = N6 // 4
    return pl.pallas_call(
        _mod_kernel,
        out_shape=jax.ShapeDtypeStruct((L, R, N6), F32),
        grid=(L, N6 // TN),
        in_specs=[pl.BlockSpec((R, D), lambda l, j: (0, 0)),
                  pl.BlockSpec((1, D, TN), lambda l, j: (l, 0, j)),
                  pl.BlockSpec((1, 1, TN), lambda l, j: (l, 0, j))],
        out_specs=pl.BlockSpec((1, R, TN), lambda l, j: (l, 0, j)),
        compiler_params=_cparams(("arbitrary", "arbitrary")),
        name="adaln_mod",
    )(c_all, w_mod, b_mod.reshape(L, 1, N6))


def _fox_cum_aug(logf, carry_row, tril3_ref, pq_ref, pk_ref):
    hi, mid, lo = _split3(logf)
    stacked = jnp.concatenate([hi.astype(BF16), mid.astype(BF16), lo.astype(BF16)], axis=0)
    cum = _dot(tril3_ref[...], stacked) + carry_row
    lane = lax.broadcasted_iota(jnp.int32, cum.shape, 1)
    cum = jnp.where(lane < H_FOX, cum, 0.0)
    fh, fm, fl = _split3(cum * LOG2E)
    a3 = (fh + pltpu.roll(fm, H_FOX, axis=1) + pltpu.roll(fl, 2 * H_FOX, axis=1)
          + jnp.where(lane == 3 * H_FOX, 1.0, 0.0)).astype(BF16)
    augq = _dot(a3, pq_ref[...]).astype(BF16)
    augk = _dot(a3, pk_ref[...]).astype(BF16)
    return cum, augq, augk


def _aug_placement():
    pq = np.zeros((LANES, H_FOX * FOX_DIM), np.float32)
    pk = np.zeros((LANES, H_FOX * FOX_DIM), np.float32)
    for h in range(H_FOX):
        base = h * FOX_DIM
        for j in range(3):
            pq[3 * H_FOX, base + j] = 1.0
            pq[j * H_FOX + h, base + 3 + j] = 1.0
            pk[j * H_FOX + h, base + j] = -1.0
            pk[3 * H_FOX, base + 3 + j] = 1.0
    return jnp.asarray(pq, BF16), jnp.asarray(pk, BF16)


def _tril3(tm):
    t = np.tril(np.ones((tm, tm), np.float32))
    return jnp.asarray(np.concatenate([t, t, t], axis=1), BF16)


def _rope_lanes(x, c, s1, s2):
    return x * c + pltpu.roll(x, LANES - ROPE_DIM // 2, axis=1) * s1 + pltpu.roll(x, ROPE_DIM // 2, axis=1) * s2


def _pre_kernel(x_ref, mod_ref, na_ref, wq_ref, wkv_ref, wkr_ref, wfq_ref, wfk_ref, wfv_ref, wfl_ref, bfl_ref,
                qn_ref, wqu_ref, kvn_ref, wfvt_ref, cq_ref, s1q_ref, s2q_ref, ck_ref, s1k_ref, s2k_ref,
                tril3_ref, pq_ref, pk_ref, cinit_ref,
                ckv_ref, kro_ref, krp_ref, kf_ref, vf_ref, lf_ref, qm_ref, qf_ref, kfb_ref, vfb_ref,
                aq_ref, ak_ref, vft_ref, carry_ref):
    @pl.when(pl.program_id(1) == 0)
    def _():
        carry_ref[...] = cinit_ref[0]

    x = x_ref[...]
    shift = mod_ref[0, 0:1, :]
    scale = mod_ref[0, 1:2, :]
    h = _rms(x, na_ref[...]) * (1.0 + scale) + shift
    hb = h.astype(BF16)

    qn = _rms(_dot(hb, wq_ref[...]), qn_ref[...]).astype(BF16)
    q = _dot(qn, wqu_ref[...])
    cq, s1q, s2q = cq_ref[...], s1q_ref[...], s2q_ref[...]
    for hh in range(H_MLA):
        sl = slice(hh * HEAD_PAD, (hh + 1) * HEAD_PAD)
        qm_ref[:, sl] = _rope_lanes(q[:, sl], cq, s1q, s2q).astype(BF16)

    ckv_ref[...] = _rms(_dot(hb, wkv_ref[...]), kvn_ref[...])
    kr = _rope_lanes(_dot(hb, wkr_ref[...]), ck_ref[...], s1k_ref[...], s2k_ref[...])
    kro_ref[...] = kr[:, :ROPE_DIM]
    krp_ref[...] = kr.astype(BF16)

    qf_ref[...] = (_dot(hb, wfq_ref[...]) * (FOX_SCALE * LOG2E)).astype(BF16)
    kf = _dot(hb, wfk_ref[...])
    kf_ref[...] = kf
    kfb_ref[...] = kf.astype(BF16)
    vf = _dot(hb, wfv_ref[...])
    vf_ref[...] = vf
    vfb_ref[...] = vf.astype(BF16)
    vft_ref[...] = _vt_rows(_nt_dot(wfvt_ref[...], hb))
    z = _dot(hb, wfl_ref[...]) + bfl_ref[...]
    logf = jnp.minimum(z, 0.0) - jnp.log1p(jnp.exp(-jnp.abs(z)))
    lane = lax.broadcasted_iota(jnp.int32, logf.shape, 1)
    logf = jnp.where(lane < H_FOX, logf, 0.0)
    lf_ref[...] = logf[:, :H_FOX]
    cum, augq, augk = _fox_cum_aug(logf, carry_ref[0:1, :], tril3_ref, pq_ref, pk_ref)
    tm = cum.shape[0]
    carry_ref[...] = jnp.broadcast_to(cum[tm - 1:tm, :], carry_ref.shape)
    aq_ref[...] = augq
    ak_ref[...] = augk


def _pre_attention(x, mod8, lw, tabs, consts, cinit, tm):
    B, T, D = x.shape
    nt = T // tm
    N = B * T
    xf = x.reshape(N, D)
    row = lambda w: pl.BlockSpec((tm, w), lambda b, i: (b * nt + i, 0))
    full = lambda a: pl.BlockSpec(a.shape, lambda b, i: (0,) * a.ndim)
    tab = pl.BlockSpec((tm, LANES), lambda b, i: (i, 0))
    weights = [lw["na"], lw["wq"], lw["wkv"], lw["wkr"], lw["wfq"], lw["wfk"], lw["wfv"], lw["wfl"], lw["bfl"],
               lw["qn"], lw["wqu"], lw["kvn"], lw["wfvt"]]
    HD = H_FOX * FOX_DIM
    outs = [(KV_LORA, F32), (ROPE_DIM, F32), (LANES, BF16), (HD, F32), (HD, F32), (H_FOX, F32),
            (H_MLA * HEAD_PAD, BF16), (HD, BF16), (HD, BF16), (HD, BF16), (HD, BF16), (HD, BF16)]
    VR = H_FOX * VT_ROWS
    return pl.pallas_call(
        _pre_kernel,
        out_shape=[jax.ShapeDtypeStruct((N, w), dt) for w, dt in outs] + [jax.ShapeDtypeStruct((B * VR, T), BF16)],
        grid=(B, nt),
        in_specs=([row(D), pl.BlockSpec((1, 8, D), lambda b, i: (b, 0, 0))] + [full(w) for w in weights]
                  + [tab] * 6 + [full(c) for c in consts]
                  + [pl.BlockSpec((1, 8, LANES), lambda b, i: (b, 0, 0))]),
        out_specs=[row(w) for w, _ in outs] + [pl.BlockSpec((VR, tm), lambda b, i: (b, i))],
        scratch_shapes=[pltpu.VMEM((8, LANES), F32)],
        compiler_params=_cparams(("arbitrary", "arbitrary")),
        name="pre_attention",
    )(xf, mod8, *weights, *tabs, *consts, cinit)


def _past_kernel(lf_ref, tril3_ref, pq_ref, pk_ref, ak_ref, last_ref, carry_ref):
    @pl.when(pl.program_id(1) == 0)
    def _():
        carry_ref[...] = jnp.zeros_like(carry_ref)

    cum, _, augk = _fox_cum_aug(lf_ref[...], carry_ref[0:1, :], tril3_ref, pq_ref, pk_ref)
    tm = cum.shape[0]
    carry_ref[...] = jnp.broadcast_to(cum[tm - 1:tm, :], carry_ref.shape)
    ak_ref[...] = augk
    last_ref[0] = carry_ref[...]


def _past_forget(logf_past, consts, tm):
    B, P, H = logf_past.shape
    nt = P // tm
    lf = jnp.pad(logf_past.reshape(B * P, H), ((0, 0), (0, LANES - H)))
    full = lambda a: pl.BlockSpec(a.shape, lambda b, i: (0,) * a.ndim)
    return pl.pallas_call(
        _past_kernel,
        out_shape=[jax.ShapeDtypeStruct((B * P, H_FOX * FOX_DIM), BF16),
                   jax.ShapeDtypeStruct((B, 8, LANES), F32)],
        grid=(B, nt),
        in_specs=[pl.BlockSpec((tm, LANES), lambda b, i: (b * nt + i, 0))] + [full(c) for c in consts],
        out_specs=[pl.BlockSpec((tm, H_FOX * FOX_DIM), lambda b, i: (b * nt + i, 0)),
                   pl.BlockSpec((1, 8, LANES), lambda b, i: (b, 0, 0))],
        scratch_shapes=[pltpu.VMEM((8, LANES), F32)],
        compiler_params=_cparams(("arbitrary", "arbitrary")),
        name="past_forget",
    )(lf, *consts)


def _kvup_kernel(ckv_ref, krp_ref, wk_ref, wvt_ref, k_ref, vt_ref):
    cb = ckv_ref[...].astype(BF16)
    k_ref[...] = _dot(jnp.concatenate([cb, krp_ref[...]], axis=1), wk_ref[...]).astype(BF16)
    vt_ref[...] = _vt_rows(_nt_dot(wvt_ref[...], cb))


def _kv_up(ckv, krp, wk, wvt, B, S, tm):
    nt = S // tm
    row = lambda w: pl.BlockSpec((tm, w), lambda b, i: (b * nt + i, 0))
    full = lambda a: pl.BlockSpec(a.shape, lambda b, i: (0,) * a.ndim)
    VR = H_MLA * VT_ROWS
    return pl.pallas_call(
        _kvup_kernel,
        out_shape=[jax.ShapeDtypeStruct((B * S, H_MLA * HEAD_PAD), BF16),
                   jax.ShapeDtypeStruct((B * VR, S), BF16)],
        grid=(B, nt),
        in_specs=[row(KV_LORA), row(LANES), full(wk), full(wvt)],
        out_specs=[row(H_MLA * HEAD_PAD), pl.BlockSpec((VR, tm), lambda b, i: (b, i))],
        compiler_params=_cparams(("arbitrary", "arbitrary")),
        name="mla_kv_up",
    )(ckv, krp, wk, wvt)


def _attn_kernel(*refs, fox, tq, tk, p0, nq):
    if fox:
        q_ref, aq_ref, k_ref, ak_ref, vt_ref, o_ref, qs_ref, sa_ref, sb_ref, m_ref, acc_ref = refs
    else:
        q_ref, k_ref, vt_ref, o_ref, sa_ref, sb_ref, m_ref, acc_ref = refs
    i = pl.program_id(2)
    q0 = p0 + i * tq
    nfull = q0 // tk

    def build_q(iq):
        if not fox:
            return None
        rs = pl.multiple_of(iq * tq, tq)
        lane = lax.broadcasted_iota(jnp.int32, (tq, LANES), 1)
        q = q_ref[pl.ds(rs, tq), :]
        aq = aq_ref[pl.ds(rs, tq), :]
        zero = jnp.zeros_like(q)
        vals = []
        for j in range(2):
            own = (lane >= j * FOX_DIM) & (lane < (j + 1) * FOX_DIM)
            vals.append(jnp.concatenate([jnp.where(own, q, zero), jnp.where(own, aq, zero)], axis=1))
            qs_ref[j] = vals[j]
        return vals

    m_ref[...] = jnp.full(m_ref.shape, NEG, F32)
    acc_ref[...] = jnp.zeros(acc_ref.shape, F32)

    def scores(kt, s_ref, iq, qvals=None):
        ks = pl.multiple_of(kt * tk, tk)
        rs = pl.multiple_of(iq * tq, tq)
        if fox:
            kj = jnp.concatenate([k_ref[pl.ds(ks, tk), :], ak_ref[pl.ds(ks, tk), :]], axis=1)
        for j in range(2):
            if fox:
                qj = qs_ref[j] if qvals is None else qvals[j]
            else:
                qj = q_ref[pl.ds(rs, tq), j * HEAD_PAD:(j + 1) * HEAD_PAD]
                kj = k_ref[pl.ds(ks, tk), j * HEAD_PAD:(j + 1) * HEAD_PAD]
            s_ref[j] = lax.dot_general(kj, qj, (((1,), (1,)), ((), ())), preferred_element_type=F32)

    def allowed(k0, q_off, nk, nqq):
        kpos = k0 + lax.broadcasted_iota(jnp.int32, (nk, nqq), 0)
        qpos = q0 + q_off + lax.broadcasted_iota(jnp.int32, (nk, nqq), 1)
        if fox:
            return kpos <= qpos
        return jnp.right_shift(kpos, CHUNK_SHIFT) <= jnp.right_shift(qpos, CHUNK_SHIFT)

    def softmax_pv_diag(kt, s_ref):
        hk = tk // 2
        ks = pl.multiple_of(kt * tk, tk)
        ok_top = allowed(ks, 0, hk, tq)
        ok_bot = allowed(ks + hk, hk, hk, hk)
        for j in range(2):
            s_top = jnp.where(ok_top, s_ref[j, 0:hk, :], NEG)
            s_bot = jnp.where(ok_bot, s_ref[j, hk:tk, hk:tq], NEG)
            m_prev = m_ref[j]
            m_top = jnp.max(s_top, axis=0, keepdims=True)
            m_bot = jnp.max(s_bot, axis=0, keepdims=True)
            m_cur = jnp.concatenate([m_top[:, :hk], jnp.maximum(m_top[:, hk:], m_bot)], axis=1)
            m_new = jnp.maximum(m_prev, m_cur)
            p_top = jnp.exp2(s_top - m_new).astype(BF16)
            p_bot = jnp.exp2(s_bot - m_new[:, hk:]).astype(BF16)
            alpha = jnp.exp2(m_prev - m_new)
            rows = slice(j * VT_ROWS, (j + 1) * VT_ROWS)
            acc = alpha * acc_ref[j] + _dot(vt_ref[rows, pl.ds(ks, hk)], p_top)
            pv_bot = _dot(vt_ref[rows, pl.ds(pl.multiple_of(ks + hk, hk), hk)], p_bot)
            acc_ref[j] = jnp.concatenate([acc[:, :hk], acc[:, hk:] + pv_bot], axis=1)
            m_ref[j] = m_new

    def softmax_pv(kt, s_ref, masked):
        if masked and tq == tk and (tk // 2) % LANES == 0:
            softmax_pv_diag(kt, s_ref)
            return
        ks = pl.multiple_of(kt * tk, tk)
        if masked:
            kpos = ks + lax.broadcasted_iota(jnp.int32, (tk, tq), 0)
            qpos = q0 + lax.broadcasted_iota(jnp.int32, (tk, tq), 1)
            if fox:
                ok = kpos <= qpos
            else:
                ok = jnp.right_shift(kpos, CHUNK_SHIFT) <= jnp.right_shift(qpos, CHUNK_SHIFT)
        for j in range(2):
            s = s_ref[j]
            if masked:
                s = jnp.where(ok, s, NEG)
            m_prev = m_ref[j]
            m_new = jnp.maximum(m_prev, jnp.max(s, axis=0, keepdims=True))
            p = jnp.exp2(s - m_new)
            alpha = jnp.exp2(m_prev - m_new)
            vt = vt_ref[j * VT_ROWS:(j + 1) * VT_ROWS, pl.ds(ks, tk)]
            acc_ref[j] = alpha * acc_ref[j] + _dot(vt, p.astype(BF16))
            m_ref[j] = m_new

    def run(x_ref, y_ref, chain):
        def pair_body(c, carry):
            kt = 2 * c
            scores(kt + 1, y_ref, i)
            softmax_pv(kt, x_ref, False)
            scores(kt + 2, x_ref, i)
            softmax_pv(kt + 1, y_ref, False)
            return carry

        lax.fori_loop(0, nfull // 2, pair_body, 0)
        odd = (nfull % 2) == 1

        def tail(masked_ref, free_ref):
            if chain:
                inext = jnp.minimum(i + 1, nq - 1)
                scores(0, free_ref, inext, build_q(inext))
            softmax_pv(nfull, masked_ref, True)

        @pl.when(odd)
        def _():
            scores(nfull, y_ref, i)
            softmax_pv(nfull - 1, x_ref, False)
            tail(y_ref, x_ref)

        @pl.when(jnp.logical_not(odd))
        def _():
            tail(x_ref, y_ref)

    if nq == 1:
        scores(0, sa_ref, i, build_q(i))
        run(sa_ref, sb_ref, False)
    else:
        @pl.when(i == 0)
        def _():
            scores(0, sa_ref, i, build_q(i))

        in_a = (((i + 1) // 2) % 2) == 0

        @pl.when(in_a)
        def _():
            run(sa_ref, sb_ref, True)

        @pl.when(jnp.logical_not(in_a))
        def _():
            run(sb_ref, sa_ref, True)

    outs = []
    for j in range(2):
        acc = acc_ref[j]
        outs.append(acc[:V_DIM] * (1.0 / acc[V_DIM:V_DIM + 1]))
    o_ref[...] = jnp.concatenate(outs, axis=0).T


def _vt_from_rows(v, B, S):
    vt = v.reshape(B, S, H_MLA, V_DIM).transpose(0, 2, 3, 1)
    vt = jnp.concatenate([vt, jnp.ones((B, H_MLA, 1, S), BF16),
                          jnp.zeros((B, H_MLA, VT_ROWS - V_DIM - 1, S), BF16)], axis=2)
    return vt.reshape(B * H_MLA * VT_ROWS, S)


def _attention(qs, ks, vt, *, fox, B, T, S, tq, tk, p0):
    H2 = H_MLA // 2
    T_in = T
    if T < LANES:
        qs = tuple(jnp.pad(q.reshape(B, T, -1), ((0, 0), (0, LANES - T), (0, 0))).reshape(B * LANES, -1)
                   for q in qs)
        T = tq = LANES
    assert tk % tq == 0 and p0 % tk == 0 and S % tk == 0 and S >= p0 + T, (tq, tk, p0, S, T)
    nq = T // tq
    assert nq == 1 or (tq == tk and p0 == 0), (tq, tk, T, p0)
    qw = LANES if fox else 2 * HEAD_PAD
    qspec = pl.BlockSpec((T, qw), lambda b, h, i: (b, h))
    kspec = pl.BlockSpec((S, qw), lambda b, h, i: (b, h))
    vspec = pl.BlockSpec((2 * VT_ROWS, S), lambda b, h, i: (b * H2 + h, 0))
    if fox:
        in_specs = [qspec, qspec, kspec, kspec, vspec]
        args = (qs[0], qs[1], ks[0], ks[1], vt)
        scratch = [pltpu.VMEM((2, tq, 2 * LANES), BF16)]
    else:
        in_specs = [qspec, kspec, vspec]
        args = (qs[0], ks[0], vt)
        scratch = []
    scratch += [pltpu.VMEM((2, tk, tq), F32), pltpu.VMEM((2, tk, tq), F32),
                pltpu.VMEM((2, 1, tq), F32), pltpu.VMEM((2, VT_ROWS, tq), F32)]
    out = pl.pallas_call(
        functools.partial(_attn_kernel, fox=fox, tq=tq, tk=tk, p0=p0, nq=nq),
        out_shape=jax.ShapeDtypeStruct((B * T, H_MLA * V_DIM), F32),
        grid=(B, H2, nq),
        in_specs=in_specs,
        out_specs=pl.BlockSpec((tq, LANES), lambda b, h, i: (b * nq + i, h)),
        scratch_shapes=scratch,
        compiler_params=_cparams(("arbitrary", "arbitrary", "arbitrary")),
        name="fox_attention" if fox else "mla_attention",
    )(*args)
    if T_in != T:
        out = out.reshape(B, T, -1)[:, :T_in].reshape(B * T_in, -1)
    return out


def _post_kernel(om_ref, of_ref, x_ref, mod_ref, onm_ref, onf_ref, wo_ref, nf_ref, wrh_ref, wrl_ref, br_ref,
                 x1_ref, hg_ref):
    o = jnp.concatenate([_rms(om_ref[...], onm_ref[...]), _rms(of_ref[...], onf_ref[...])], axis=1)
    mix = _dot(o.astype(BF16), wo_ref[...])
    gate1 = mod_ref[0, 2:3, :]
    shift2 = mod_ref[0, 3:4, :]
    scale2 = mod_ref[0, 4:5, :]
    x1 = x_ref[...] + gate1 * mix
    x1_ref[...] = x1
    h2 = _rms(x1, nf_ref[...]) * (1.0 + scale2) + shift2
    hh = h2.astype(BF16)
    hg_ref[:, :h2.shape[1]] = hh.astype(F32)
    hl = (h2 - hh.astype(F32)).astype(BF16)
    wrh = wrh_ref[...]
    logits = _dot(hh, wrh) + _dot(hl, wrh) + _dot(hh, wrl_ref[...]) + br_ref[...]

    lane = lax.broadcasted_iota(jnp.int32, logits.shape, 1).astype(F32)
    big = float(LANES)
    gmask = lane < N_GROUPS
    gl = jnp.where(gmask, logits, NEG)
    gmax = jnp.max(gl, axis=1, keepdims=True)
    gsum = jnp.sum(jnp.where(gmask, jnp.exp(gl - gmax), 0.0), axis=1, keepdims=True)
    g_p = 1.0 / gsum
    g_idx = jnp.min(jnp.where(gl == gmax, lane, big), axis=1, keepdims=True)
    lo = ROUTER_OFF + g_idx * EXPERTS_PER_GROUP
    emask = (lane >= lo) & (lane < lo + EXPERTS_PER_GROUP)
    el = jnp.where(emask, logits, NEG)
    emax = jnp.max(el, axis=1, keepdims=True)
    ee = jnp.where(emask, jnp.exp(el - emax), 0.0)
    ep = ee / jnp.sum(ee, axis=1, keepdims=True)
    p1 = jnp.max(ep, axis=1, keepdims=True)
    i1 = jnp.min(jnp.where(emask & (ep == p1), lane, big), axis=1, keepdims=True)
    rest = emask & (lane != i1)
    ep2 = jnp.where(rest, ep, -1.0)
    p2 = jnp.max(ep2, axis=1, keepdims=True)
    i2 = jnp.min(jnp.where(rest & (ep2 == p2), lane, big), axis=1, keepdims=True)
    den = p1 + p2
    gate = (jnp.where(lane == i1, g_p * (p1 / den), 0.0) + jnp.where(lane == i2, g_p * (p2 / den), 0.0))
    g8 = jnp.zeros_like(gate)
    for g in range(N_GROUPS):
        g8 = g8 + pltpu.roll(gate, LANES - (ROUTER_OFF + g * EXPERTS_PER_GROUP), axis=1)
    d = h2.shape[1]
    hg_ref[:, d:] = jnp.where(lane < EXPERTS_PER_GROUP, g8, jnp.where(lane == GIDX_LANE, g_idx, 0.0))


def _post_attention(om, of, x, mod8, lw, tm):
    B, T, D = x.shape
    nt = T // tm
    N = B * T
    row = lambda w: pl.BlockSpec((tm, w), lambda b, i: (b * nt + i, 0))
    full = lambda a: pl.BlockSpec(a.shape, lambda b, i: (0,) * a.ndim)
    weights = [lw["onm"], lw["onf"], lw["wo"], lw["nf"], lw["wrh"], lw["wrl"], lw["br"]]
    HV = H_MLA * V_DIM
    return pl.pallas_call(
        _post_kernel,
        out_shape=[jax.ShapeDtypeStruct((N, D), F32), jax.ShapeDtypeStruct((N, D + LANES), F32)],
        grid=(B, nt),
        in_specs=[row(HV), row(HV), row(D), pl.BlockSpec((1, 8, D), lambda b, i: (b, 0, 0))]
                 + [full(w) for w in weights],
        out_specs=[row(D), row(D + LANES)],
        compiler_params=_cparams(("arbitrary", "arbitrary")),
        name="post_attention",
    )(om, of, x.reshape(N, D), mod8, *weights)


def _row_copies(n, make, start):
    def body(r, c):
        cp = make(r)
        cp.start() if start else cp.wait()
        return c
    lax.fori_loop(0, n, body, 0, unroll=8)


def _dispatch_kernel(pos_ref, h_ref, hs_in, hs_out, sem, *, tm):
    del hs_in

    def copy(r, start):
        dst = pos_ref[0, 0, r] if start else 0
        return pltpu.make_async_copy(h_ref.at[pl.ds(r, 1), :], hs_out.at[pl.ds(dst, 1), :], sem.at[0])

    _row_copies(tm, lambda r: copy(r, True), True)
    _row_copies(tm, lambda r: copy(r, False), False)


def _dispatch(hg, pos3, R, tm):
    N, W = hg.shape
    return pl.pallas_call(
        functools.partial(_dispatch_kernel, tm=tm),
        out_shape=jax.ShapeDtypeStruct((R, W), F32),
        grid=(N // tm,),
        in_specs=[pl.BlockSpec((1, 1, tm), lambda t: (t, 0, 0), memory_space=pltpu.SMEM),
                  pl.BlockSpec((tm, W), lambda t: (t, 0)), pl.BlockSpec(memory_space=pl.ANY)],
        out_specs=pl.BlockSpec(memory_space=pl.ANY),
        scratch_shapes=[pltpu.SemaphoreType.DMA((1,))],
        input_output_aliases={2: 0},
        compiler_params=_cparams(("arbitrary",)),
        name="moe_dispatch",
    )(pos3, hg, jnp.zeros((R, W), F32))


def _moe_kernel(tg_ref, hs_ref, wg_ref, wu_ref, wd_ref, ys_ref, *, d):
    del tg_ref
    rows = hs_ref[...]
    h = rows[:, :d].astype(BF16)
    g8 = rows[:, d:]
    lane = lax.broadcasted_iota(jnp.int32, g8.shape, 1)
    acc = jnp.zeros((rows.shape[0], d), F32)
    for j in range(EXPERTS_PER_GROUP):
        a = _dot(h, wg_ref[j])
        u = _dot(h, wu_ref[j])
        gcol = jnp.sum(jnp.where(lane == j, g8, 0.0), axis=1, keepdims=True)
        act = (a * (1.0 / (1.0 + jnp.exp(-a)))) * u * gcol
        acc = acc + _dot(act.astype(BF16), wd_ref[j])
    ys_ref[...] = acc


def _moe(hg, lw, tm, tmd):
    N, W = hg.shape
    d = W - LANES
    R = N + N_GROUPS * tm
    nt = R // tm
    grp = hg[:, d + GIDX_LANE].astype(jnp.int32)
    onehot = (grp[:, None] == jnp.arange(N_GROUPS, dtype=jnp.int32)[None, :]).astype(jnp.int32)
    counts = jnp.sum(onehot, axis=0)
    padded = ((counts + tm - 1) // tm) * tm
    ends = jnp.cumsum(padded)
    starts = ends - padded
    rank = jnp.sum((jnp.cumsum(onehot, axis=0) - 1) * onehot, axis=1)
    pos = jnp.sum(onehot * starts[None, :], axis=1) + rank
    tile_first = jnp.arange(nt, dtype=jnp.int32) * tm
    tile_group = jnp.minimum(jnp.sum((tile_first[:, None] >= ends[None, :]).astype(jnp.int32), axis=1),
                             N_GROUPS - 1)

    hs = _dispatch(hg, pos.reshape(N // tmd, 1, tmd), R, tmd)
    wspec = lambda a: pl.BlockSpec((EXPERTS_PER_GROUP,) + a.shape[1:], lambda t, tg: (tg[t], 0, 0))
    grid_spec = pltpu.PrefetchScalarGridSpec(
        num_scalar_prefetch=1,
        grid=(nt,),
        in_specs=[pl.BlockSpec((tm, W), lambda t, tg: (t, 0)),
                  wspec(lw["wg"]), wspec(lw["wu"]), wspec(lw["wd"])],
        out_specs=pl.BlockSpec((tm, d), lambda t, tg: (t, 0)),
    )
    ys = pl.pallas_call(
        functools.partial(_moe_kernel, d=d),
        out_shape=jax.ShapeDtypeStruct((R, d), F32),
        grid_spec=grid_spec,
        compiler_params=_cparams(("arbitrary",)),
        name="moe_experts",
    )(tile_group, hs, lw["wg"], lw["wu"], lw["wd"])
    return ys, pos


def _residual_kernel(pos_ref, nxt_ref, x_ref, mod_ref, ys_hbm, o_ref, ybuf, sem, *, tm):
    nt = pl.num_programs(1)
    step = pl.program_id(0) * nt + pl.program_id(1)
    total = pl.num_programs(0) * nt
    slot = step % 2

    def copy(idx_ref, r, s, start):
        src = idx_ref[0, 0, r] if start else 0
        return pltpu.make_async_copy(ys_hbm.at[pl.ds(src, 1), :], ybuf.at[s, pl.ds(r, 1), :], sem.at[s])

    @pl.when(step == 0)
    def _():
        _row_copies(tm, lambda r: copy(pos_ref, r, 0, True), True)

    @pl.when(step + 1 < total)
    def _():
        _row_copies(tm, lambda r: copy(nxt_ref, r, 1 - slot, True), True)

    _row_copies(tm, lambda r: copy(pos_ref, r, slot, False), False)
    o_ref[...] = x_ref[...] + mod_ref[0, 5:6, :] * ybuf[slot]


def _residual(x1, ys, pos, mod8, tm):
    B = mod8.shape[0]
    N, D = x1.shape
    nt = N // B // tm
    ns = N // tm
    pos3 = pos.reshape(ns, 1, tm)
    row = pl.BlockSpec((tm, D), lambda b, i: (b * nt + i, 0))
    idx = lambda f: pl.BlockSpec((1, 1, tm), f, memory_space=pltpu.SMEM)
    return pl.pallas_call(
        functools.partial(_residual_kernel, tm=tm),
        out_shape=jax.ShapeDtypeStruct((N, D), F32),
        grid=(B, nt),
        in_specs=[idx(lambda b, i: (b * nt + i, 0, 0)),
                  idx(lambda b, i: (jnp.minimum(b * nt + i + 1, ns - 1), 0, 0)),
                  row, pl.BlockSpec((1, 8, D), lambda b, i: (b, 0, 0)), pl.BlockSpec(memory_space=pl.ANY)],
        out_specs=row,
        scratch_shapes=[pltpu.VMEM((2, tm, D), F32), pltpu.SemaphoreType.DMA((2,))],
        compiler_params=_cparams(("arbitrary", "arbitrary")),
        name="ffn_residual",
    )(pos3, pos3, x1, mod8, ys)


def _norm_kernel(x_ref, g_ref, o_ref):
    o_ref[...] = _rms(x_ref[...], g_ref[...])


def _final_norm(x, g, tm):
    B, T, D = x.shape
    N = B * T
    return pl.pallas_call(
        _norm_kernel,
        out_shape=jax.ShapeDtypeStruct((N, D), F32),
        grid=(N // tm,),
        in_specs=[pl.BlockSpec((tm, D), lambda i: (i, 0)), pl.BlockSpec((1, D), lambda i: (0, 0))],
        out_specs=pl.BlockSpec((tm, D), lambda i: (i, 0)),
        compiler_params=_cparams(("arbitrary",)),
        name="final_norm",
    )(x.reshape(N, D), g.reshape(1, D)).reshape(B, T, D)


def _rope_tables(p0, T):
    half = ROPE_DIM // 2
    inv = ROPE_THETA ** (-jnp.arange(half, dtype=F32) / half)
    ang = (p0 + jnp.arange(T, dtype=jnp.int32)).astype(F32)[:, None] * inv[None, :]
    cos, sin = jnp.cos(ang), jnp.sin(ang)
    z = lambda w: jnp.zeros((T, w), F32)
    one = jnp.ones((T, NOPE_DIM), F32)
    tail = z(HEAD_PAD - NOPE_DIM - ROPE_DIM)
    qs = MLA_SCALE * LOG2E
    cq = jnp.concatenate([one, cos, cos, tail], axis=1) * qs
    s1q = jnp.concatenate([z(NOPE_DIM), -sin, z(half), tail], axis=1) * qs
    s2q = jnp.concatenate([z(NOPE_DIM), z(half), sin, tail], axis=1) * qs
    kt = z(LANES - ROPE_DIM)
    ck = jnp.concatenate([cos, cos, kt], axis=1)
    s1k = jnp.concatenate([-sin, z(half), kt], axis=1)
    s2k = jnp.concatenate([z(half), sin, kt], axis=1)
    return cq, s1q, s2q, ck, s1k, s2k


def _layer_weights(l, p):
    D = p["w_in"].shape[1]
    w_in = p["w_in"][l]
    o = np.cumsum([0, Q_LORA, KV_LORA, ROPE_DIM, H_FOX * FOX_DIM, H_FOX * FOX_DIM, H_FOX * FOX_DIM, H_FOX])
    seg = lambda i: w_in[:, o[i]:o[i + 1]]
    padl = lambda w: jnp.pad(w, ((0, 0), (0, LANES - w.shape[1])))
    r1 = lambda v: v.reshape(1, -1).astype(F32)
    wqu = p["w_q_up"][l].reshape(Q_LORA, H_MLA, NOPE_DIM + ROPE_DIM)
    wqu = jnp.pad(wqu, ((0, 0), (0, 0), (0, HEAD_PAD - NOPE_DIM - ROPE_DIM))).reshape(Q_LORA, H_MLA * HEAD_PAD)
    wkv = p["w_kv_up"][l].reshape(KV_LORA, H_MLA, NOPE_DIM + V_DIM)
    wk_c = jnp.pad(wkv[:, :, :NOPE_DIM], ((0, 0), (0, 0), (0, HEAD_PAD - NOPE_DIM)))
    place = np.zeros((LANES, H_MLA, HEAD_PAD), np.float32)
    for j in range(ROPE_DIM):
        place[j, :, NOPE_DIM + j] = 1.0
    wk = jnp.concatenate([wk_c.reshape(KV_LORA, H_MLA * HEAD_PAD),
                          jnp.asarray(place.reshape(LANES, H_MLA * HEAD_PAD))], axis=0)
    wv = wkv[:, :, NOPE_DIM:].reshape(KV_LORA, H_MLA * V_DIM)
    wr = padl(jnp.concatenate([p["w_router_group"][l], p["w_router_expert"][l]], axis=1))
    wrh = wr.astype(BF16)
    wrl = (wr - wrh.astype(F32)).astype(BF16)
    br = padl(jnp.concatenate([p["b_router_group"][l], p["b_router_expert"][l]]).reshape(1, -1)).astype(F32)
    return dict(
        na=r1(p["norm_attn"][l]), wq=seg(0).astype(BF16), wkv=seg(1).astype(BF16), wkr=padl(seg(2)).astype(BF16),
        wfq=seg(3).astype(BF16), wfk=seg(4).astype(BF16), wfv=seg(5).astype(BF16), wfl=padl(seg(6)).astype(BF16),
        bfl=padl(r1(p["b_forget"][l])), qn=r1(p["q_norm"][l]), wqu=wqu.astype(BF16), kvn=r1(p["kv_norm"][l]),
        wk=wk.astype(BF16), wvt=wv.T.astype(BF16), wfvt=seg(5).T.astype(BF16),
        onm=r1(p["out_norm_mla"][l]), onf=r1(p["out_norm_fox"][l]), wo=p["w_out"][l].astype(BF16),
        nf=r1(p["norm_ffn"][l]), wrh=wrh, wrl=wrl, br=br,
        wg=p["w_gate"][l].astype(BF16), wu=p["w_up"][l].astype(BF16), wd=p["w_down"][l].astype(BF16),
    )


def _tile(n, pref):
    return pref if n % pref == 0 else n


def _layer(x, mod8, lw, past, cfg):
    B, T, D = x.shape
    tm = cfg["tm"]
    aug_consts = (cfg["tril3"], cfg["pq"], cfg["pk"])
    if past is None:
        P = 0
        cinit = jnp.zeros((B, 8, LANES), F32)
    else:
        P = past[0].shape[1]
        augk_past, cinit = _past_forget(past[4].astype(F32), (cfg["tril3_past"], cfg["pq"], cfg["pk"]),
                                        cfg["tm_past"])
    (ckv, kro, krp, kf, vf, lf, qm, qf, kfb, vfb, aq, ak, vft) = _pre_attention(
        x, mod8, lw, cfg["tabs"], aug_consts, cinit, tm)
    new_rows = (ckv.reshape(B, T, KV_LORA), kro.reshape(B, T, ROPE_DIM),
                kf.reshape(B, T, H_FOX, FOX_DIM), vf.reshape(B, T, H_FOX, FOX_DIM), lf.reshape(B, T, H_FOX))

    S = P + T
    if past is None:
        ckv_all, krp_all, kfb_all, ak_all = ckv, krp, kfb, ak
        Sp = S
    else:
        Sp = -(-S // cfg["tk"]) * cfg["tk"]
        HD = H_FOX * FOX_DIM

        def cat(old, new, w):
            a = jnp.concatenate([old.reshape(B, P, w).astype(new.dtype), new.reshape(B, T, w)], axis=1)
            return jnp.pad(a, ((0, 0), (0, Sp - S), (0, 0))).reshape(B * Sp, w)

        ckv_all = cat(past[0], ckv, KV_LORA)
        krp_all = cat(jnp.pad(past[1], ((0, 0), (0, 0), (0, LANES - ROPE_DIM))), krp, LANES)
        kfb_all = cat(past[2], kfb, HD)
        vft = _vt_from_rows(cat(past[3], vfb, HD), B, Sp)
        ak_all = cat(augk_past, ak, HD)
    km, vmt = _kv_up(ckv_all, krp_all, lw["wk"], lw["wvt"], B, Sp, _tile(Sp, 512))
    att = dict(B=B, T=T, S=Sp, tq=cfg["tq"], tk=cfg["tk"], p0=P)
    om = _attention((qm,), (km,), vmt, fox=False, **att)
    of = _attention((qf, aq), (kfb_all, ak_all), vft, fox=True, **att)
    x1, hg = _post_attention(om, of, x, mod8, lw, tm)
    ys, pos = _moe(hg, lw, cfg["tms"], cfg["tms"])
    x2 = _residual(x1, ys, pos, mod8, tm).reshape(B, T, D)
    return x2, new_rows


def _group_cfg(B, T, P, pq, pk):
    tm = _tile(T, 256)
    tq = _tile(T, 512)
    tk = tq if P == 0 else (512 if P % 512 == 0 else LANES)
    cfg = dict(tm=tm, tq=tq, tk=tk, tms=_tile(B * T, 512), tril3=_tril3(tm), pq=pq, pk=pk,
               tabs=_rope_tables(P, T))
    if P:
        cfg["tm_past"] = _tile(P, 256)
        cfg["tril3_past"] = _tril3(cfg["tm_past"])
    return cfg


def kernel(x_prompt, x_sample, cache_mla_ckv, cache_mla_krope, cache_fox_k, cache_fox_v, cache_fox_logf, c_prompt, c_sample, w_mod, b_mod, norm_attn, w_in, b_forget, q_norm, w_q_up, kv_norm, w_kv_up, out_norm_mla, out_norm_fox, w_out, norm_ffn, w_router_group, b_router_group, w_router_expert, b_router_expert, w_gate, w_up, w_down, final_norm):
    params = dict(norm_attn=norm_attn, w_in=w_in, b_forget=b_forget, q_norm=q_norm, w_q_up=w_q_up,
                  kv_norm=kv_norm, w_kv_up=w_kv_up, out_norm_mla=out_norm_mla, out_norm_fox=out_norm_fox,
                  w_out=w_out, norm_ffn=norm_ffn, w_router_group=w_router_group, b_router_group=b_router_group,
                  w_router_expert=w_router_expert, b_router_expert=b_router_expert,
                  w_gate=w_gate, w_up=w_up, w_down=w_down)
    L = w_mod.shape[0]
    Bp, Tp, D = x_prompt.shape
    Bs, Ts, _ = x_sample.shape
    P = cache_mla_ckv.shape[2]

    R = -(-(Bp + Bs) // 8) * 8
    c_all = jnp.pad(jnp.concatenate([c_prompt, c_sample], axis=0), ((0, R - Bp - Bs), (0, 0)))
    mod = _mod_all(c_all, w_mod, b_mod).reshape(L, R, 6, D)
    mod = jnp.pad(mod, ((0, 0), (0, 0), (0, 2), (0, 0)))

    pq, pk = _aug_placement()
    cfg_p = _group_cfg(Bp, Tp, 0, pq, pk)
    cfg_s = _group_cfg(Bs, Ts, P, pq, pk)

    y_p, y_s = x_prompt, x_sample
    rows_p, rows_s = [], []
    for l in range(L):
        lw = _layer_weights(l, params)
        y_p, r_p = _layer(y_p, mod[l, :Bp], lw, None, cfg_p)
        past = (cache_mla_ckv[l], cache_mla_krope[l], cache_fox_k[l], cache_fox_v[l], cache_fox_logf[l])
        y_s, r_s = _layer(y_s, mod[l, Bp:Bp + Bs], lw, past, cfg_s)
        rows_p.append(r_p)
        rows_s.append(r_s)

    y_prompt = _final_norm(y_p, final_norm, _tile(Bp * Tp, 512))
    y_sample = _final_norm(y_s, final_norm, _tile(Bs * Ts, 512))
    stack = lambda rows, k: jnp.stack([r[k] for r in rows])
    return (y_prompt, y_sample,
            stack(rows_p, 0), stack(rows_p, 1), stack(rows_p, 2), stack(rows_p, 3), stack(rows_p, 4),
            stack(rows_s, 0), stack(rows_s, 1), stack(rows_s, 2), stack(rows_s, 3), stack(rows_s, 4))
```

```python
import functools

import numpy as np
import jax
import jax.numpy as jnp
from jax import lax
from jax.experimental import pallas as pl
from jax.experimental.pallas import tpu as pltpu

CHUNK = 64
CHUNK_SHIFT = 6
EPS = 1e-6
H_MLA = 8
NOPE_DIM = 64
ROPE_DIM = 32
V_DIM = 64
Q_LORA = 384
KV_LORA = 256
ROPE_THETA = 10000.0
MLA_SCALE = (NOPE_DIM + ROPE_DIM) ** -0.5
H_FOX = 8
FOX_DIM = 64
FOX_SCALE = FOX_DIM ** -0.5
N_GROUPS = 4
EXPERTS_PER_GROUP = 8
N_EXPERTS = N_GROUPS * EXPERTS_PER_GROUP
D_EXPERT = 256

LANES = 128
HEAD_PAD = LANES
VT_ROWS = 80
ROUTER_OFF = N_GROUPS
GIDX_LANE = 16
NEG = -1e30
LOG2E = 1.4426950408889634
VMEM_LIMIT = 56 * 1024 * 1024

F32 = jnp.float32
BF16 = jnp.bfloat16


def _cparams(sem):
    return pltpu.CompilerParams(dimension_semantics=sem, vmem_limit_bytes=VMEM_LIMIT)


def _rms(x, g):
    return x * lax.rsqrt(jnp.mean(x * x, axis=-1, keepdims=True) + EPS) * g


def _dot(a, b):
    return jnp.dot(a, b, preferred_element_type=F32)


def _split3(x):
    hi = x.astype(BF16).astype(F32)
    r = x - hi
    mid = r.astype(BF16).astype(F32)
    lo = (r - mid).astype(BF16).astype(F32)
    return hi, mid, lo


def _vt_rows(vt):
    n = vt.shape[1]
    r = lax.broadcasted_iota(jnp.int32, (VT_ROWS - V_DIM, n), 0)
    tail = jnp.where(r == 0, 1.0, 0.0).astype(BF16)
    vb = vt.astype(BF16)
    parts = []
    for h in range(H_MLA):
        parts += [vb[h * V_DIM:(h + 1) * V_DIM], tail]
    return jnp.concatenate(parts, axis=0)


def _nt_dot(a, b):
    return lax.dot_general(a, b, (((1,), (1,)), ((), ())), preferred_element_type=F32)


def _mod_kernel(c_ref, w_ref, b_ref, o_ref):
    o_ref[0] = _dot(c_ref[...].astype(BF16), w_ref[0].astype(BF16)) + b_ref[0]


def _mod_all(c_all, w_mod, b_mod):
    L, D, N6 = w_mod.shape
    R = c_all.shape[0]
    TN = N6 // 4
    return pl.pallas_call(
        _mod_kernel,
        out_shape=jax.ShapeDtypeStruct((L, R, N6), F32),
        grid=(L, N6 // TN),
        in_specs=[pl.BlockSpec((R, D), lambda l, j: (0, 0)),
                  pl.BlockSpec((1, D, TN), lambda l, j: (l, 0, j)),
                  pl.BlockSpec((1, 1, TN), lambda l, j: (l, 0, j))],
        out_specs=pl.BlockSpec((1, R, TN), lambda l, j: (l, 0, j)),
        compiler_params=_cparams(("arbitrary", "arbitrary")),
        name="adaln_mod",
    )(c_all, w_mod, b_mod.reshape(L, 1, N6))


def _fox_cum_aug(logf, carry_row, tril3_ref, pq_ref, pk_ref):
    hi, mid, lo = _split3(logf)
    stacked = jnp.concatenate([hi.astype(BF16), mid.astype(BF16), lo.astype(BF16)], axis=0)
    cum = _dot(tril3_ref[...], stacked) + carry_row
    lane = lax.broadcasted_iota(jnp.int32, cum.shape, 1)
    cum = jnp.where(lane < H_FOX, cum, 0.0)
    fh, fm, fl = _split3(cum * LOG2E)
    a3 = (fh + pltpu.roll(fm, H_FOX, axis=1) + pltpu.roll(fl, 2 * H_FOX, axis=1)
          + jnp.where(lane == 3 * H_FOX, 1.0, 0.0)).astype(BF16)
    augq = _dot(a3, pq_ref[...]).astype(BF16)
    augk = _dot(a3, pk_ref[...]).astype(BF16)
    return cum, augq, augk


def _aug_placement():
    pq = np.zeros((LANES, H_FOX * FOX_DIM), np.float32)
    pk = np.zeros((LANES, H_FOX * FOX_DIM), np.float32)
    for h in range(H_FOX):
        base = h * FOX_DIM
        for j in range(3):
            pq[3 * H_FOX, base + j] = 1.0
            pq[j * H_FOX + h, base + 3 + j] = 1.0
            pk[j * H_FOX + h, base + j] = -1.0
            pk[3 * H_FOX, base + 3 + j] = 1.0
    return jnp.asarray(pq, BF16), jnp.asarray(pk, BF16)


def _tril3(tm):
    t = np.tril(np.ones((tm, tm), np.float32))
    return jnp.asarray(np.concatenate([t, t, t], axis=1), BF16)


def _rope_lanes(x, c, s1, s2):
    return x * c + pltpu.roll(x, LANES - ROPE_DIM // 2, axis=1) * s1 + pltpu.roll(x, ROPE_DIM // 2, axis=1) * s2


def _pre_kernel(x_ref, mod_ref, na_ref, wq_ref, wkv_ref, wkr_ref, wfq_ref, wfk_ref, wfv_ref, wfl_ref, bfl_ref,
                qn_ref, wqu_ref, kvn_ref, wfvt_ref, cq_ref, s1q_ref, s2q_ref, ck_ref, s1k_ref, s2k_ref,
                tril3_ref, pq_ref, pk_ref, cinit_ref,
                ckv_ref, kro_ref, krp_ref, kf_ref, vf_ref, lf_ref, qm_ref, qf_ref, kfb_ref, vfb_ref,
                aq_ref, ak_ref, vft_ref, carry_ref):
    @pl.when(pl.program_id(1) == 0)
    def _():
        carry_ref[...] = cinit_ref[0]

    x = x_ref[...]
    shift = mod_ref[0, 0:1, :]
    scale = mod_ref[0, 1:2, :]
    h = _rms(x, na_ref[...]) * (1.0 + scale) + shift
    hb = h.astype(BF16)

    qn = _rms(_dot(hb, wq_ref[...]), qn_ref[...]).astype(BF16)
    q = _dot(qn, wqu_ref[...])
    cq, s1q, s2q = cq_ref[...], s1q_ref[...], s2q_ref[...]
    for hh in range(H_MLA):
        sl = slice(hh * HEAD_PAD, (hh + 1) * HEAD_PAD)
        qm_ref[:, sl] = _rope_lanes(q[:, sl], cq, s1q, s2q).astype(BF16)

    ckv_ref[...] = _rms(_dot(hb, wkv_ref[...]), kvn_ref[...])
    kr = _rope_lanes(_dot(hb, wkr_ref[...]), ck_ref[...], s1k_ref[...], s2k_ref[...])
    kro_ref[...] = kr[:, :ROPE_DIM]
    krp_ref[...] = kr.astype(BF16)

    qf_ref[...] = (_dot(hb, wfq_ref[...]) * (FOX_SCALE * LOG2E)).astype(BF16)
    kf = _dot(hb, wfk_ref[...])
    kf_ref[...] = kf
    kfb_ref[...] = kf.astype(BF16)
    vf = _dot(hb, wfv_ref[...])
    vf_ref[...] = vf
    vfb_ref[...] = vf.astype(BF16)
    vft_ref[...] = _vt_rows(_nt_dot(wfvt_ref[...], hb))
    z = _dot(hb, wfl_ref[...]) + bfl_ref[...]
    logf = jnp.minimum(z, 0.0) - jnp.log1p(jnp.exp(-jnp.abs(z)))
    lane = lax.broadcasted_iota(jnp.int32, logf.shape, 1)
    logf = jnp.where(lane < H_FOX, logf, 0.0)
    lf_ref[...] = logf[:, :H_FOX]
    cum, augq, augk = _fox_cum_aug(logf, carry_ref[0:1, :], tril3_ref, pq_ref, pk_ref)
    tm = cum.shape[0]
    carry_ref[...] = jnp.broadcast_to(cum[tm - 1:tm, :], carry_ref.shape)
    aq_ref[...] = augq
    ak_ref[...] = augk


def _pre_attention(x, mod8, lw, tabs, consts, cinit, tm):
    B, T, D = x.shape
    nt = T // tm
    N = B * T
    xf = x.reshape(N, D)
    row = lambda w: pl.BlockSpec((tm, w), lambda b, i: (b * nt + i, 0))
    full = lambda a: pl.BlockSpec(a.shape, lambda b, i: (0,) * a.ndim)
    tab = pl.BlockSpec((tm, LANES), lambda b, i: (i, 0))
    weights = [lw["na"], lw["wq"], lw["wkv"], lw["wkr"], lw["wfq"], lw["wfk"], lw["wfv"], lw["wfl"], lw["bfl"],
               lw["qn"], lw["wqu"], lw["kvn"], lw["wfvt"]]
    HD = H_FOX * FOX_DIM
    outs = [(KV_LORA, F32), (ROPE_DIM, F32), (LANES, BF16), (HD, F32), (HD, F32), (H_FOX, F32),
            (H_MLA * HEAD_PAD, BF16), (HD, BF16), (HD, BF16), (HD, BF16), (HD, BF16), (HD, BF16)]
    VR = H_FOX * VT_ROWS
    return pl.pallas_call(
        _pre_kernel,
        out_shape=[jax.ShapeDtypeStruct((N, w), dt) for w, dt in outs] + [jax.ShapeDtypeStruct((B * VR, T), BF16)],
        grid=(B, nt),
        in_specs=([row(D), pl.BlockSpec((1, 8, D), lambda b, i: (b, 0, 0))] + [full(w) for w in weights]
                  + [tab] * 6 + [full(c) for c in consts]
                  + [pl.BlockSpec((1, 8, LANES), lambda b, i: (b, 0, 0))]),
        out_specs=[row(w) for w, _ in outs] + [pl.BlockSpec((VR, tm), lambda b, i: (b, i))],
        scratch_shapes=[pltpu.VMEM((8, LANES), F32)],
        compiler_params=_cparams(("arbitrary", "arbitrary")),
        name="pre_attention",
    )(xf, mod8, *weights, *tabs, *consts, cinit)


def _past_kernel(lf_ref, tril3_ref, pq_ref, pk_ref, ak_ref, last_ref, carry_ref):
    @pl.when(pl.program_id(1) == 0)
    def _():
        carry_ref[...] = jnp.zeros_like(carry_ref)

    cum, _, augk = _fox_cum_aug(lf_ref[...], carry_ref[0:1, :], tril3_ref, pq_ref, pk_ref)
    tm = cum.shape[0]
    carry_ref[...] = jnp.broadcast_to(cum[tm - 1:tm, :], carry_ref.shape)
    ak_ref[...] = augk
    last_ref[0] = carry_ref[...]


def _past_forget(logf_past, consts, tm):
    B, P, H = logf_past.shape
    nt = P // tm
    lf = jnp.pad(logf_past.reshape(B * P, H), ((0, 0), (0, LANES - H)))
    full = lambda a: pl.BlockSpec(a.shape, lambda b, i: (0,) * a.ndim)
    return pl.pallas_call(
        _past_kernel,
        out_shape=[jax.ShapeDtypeStruct((B * P, H_FOX * FOX_DIM), BF16),
                   jax.ShapeDtypeStruct((B, 8, LANES), F32)],
        grid=(B, nt),
        in_specs=[pl.BlockSpec((tm, LANES), lambda b, i: (b * nt + i, 0))] + [full(c) for c in consts],
        out_specs=[pl.BlockSpec((tm, H_FOX * FOX_DIM), lambda b, i: (b * nt + i, 0)),
                   pl.BlockSpec((1, 8, LANES), lambda b, i: (b, 0, 0))],
        scratch_shapes=[pltpu.VMEM((8, LANES), F32)],
        compiler_params=_cparams(("arbitrary", "arbitrary")),
        name="past_forget",
    )(lf, *consts)


def _kvup_kernel(ckv_ref, krp_ref, wk_ref, wvt_ref, k_ref, vt_ref):
    cb = ckv_ref[...].astype(BF16)
    k_ref[...] = _dot(jnp.concatenate([cb, krp_ref[...]], axis=1), wk_ref[...]).astype(BF16)
    vt_ref[...] = _vt_rows(_nt_dot(wvt_ref[...], cb))


def _kv_up(ckv, krp, wk, wvt, B, S, tm):
    nt = S // tm
    row = lambda w: pl.BlockSpec((tm, w), lambda b, i: (b * nt + i, 0))
    full = lambda a: pl.BlockSpec(a.shape, lambda b, i: (0,) * a.ndim)
    VR = H_MLA * VT_ROWS
    return pl.pallas_call(
        _kvup_kernel,
        out_shape=[jax.ShapeDtypeStruct((B * S, H_MLA * HEAD_PAD), BF16),
                   jax.ShapeDtypeStruct((B * VR, S), BF16)],
        grid=(B, nt),
        in_specs=[row(KV_LORA), row(LANES), full(wk), full(wvt)],
        out_specs=[row(H_MLA * HEAD_PAD), pl.BlockSpec((VR, tm), lambda b, i: (b, i))],
        compiler_params=_cparams(("arbitrary", "arbitrary")),
        name="mla_kv_up",
    )(ckv, krp, wk, wvt)


def _attn_kernel(*refs, fox, tq, tk, p0, nq):
    if fox:
        q_ref, aq_ref, k_ref, ak_ref, vt_ref, o_ref, qs_ref, sa_ref, sb_ref, m_ref, acc_ref = refs
    else:
        q_ref, k_ref, vt_ref, o_ref, sa_ref, sb_ref, m_ref, acc_ref = refs
    i = pl.program_id(2)
    q0 = p0 + i * tq
    nfull = q0 // tk

    def build_q(iq):
        if not fox:
            return None
        rs = pl.multiple_of(iq * tq, tq)
        lane = lax.broadcasted_iota(jnp.int32, (tq, LANES), 1)
        q = q_ref[pl.ds(rs, tq), :]
        aq = aq_ref[pl.ds(rs, tq), :]
        zero = jnp.zeros_like(q)
        vals = []
        for j in range(2):
            own = (lane >= j * FOX_DIM) & (lane < (j + 1) * FOX_DIM)
            vals.append(jnp.concatenate([jnp.where(own, q, zero), jnp.where(own, aq, zero)], axis=1))
            qs_ref[j] = vals[j]
        return vals

    m_ref[...] = jnp.full(m_ref.shape, NEG, F32)
    acc_ref[...] = jnp.zeros(acc_ref.shape, F32)

    def scores(kt, s_ref, iq, qvals=None):
        ks = pl.multiple_of(kt * tk, tk)
        rs = pl.multiple_of(iq * tq, tq)
        if fox:
            kj = jnp.concatenate([k_ref[pl.ds(ks, tk), :], ak_ref[pl.ds(ks, tk), :]], axis=1)
        for j in range(2):
            if fox:
                qj = qs_ref[j] if qvals is None else qvals[j]
            else:
                qj = q_ref[pl.ds(rs, tq), j * HEAD_PAD:(j + 1) * HEAD_PAD]
                kj = k_ref[pl.ds(ks, tk), j * HEAD_PAD:(j + 1) * HEAD_PAD]
            s_ref[j] = lax.dot_general(kj, qj, (((1,), (1,)), ((), ())), preferred_element_type=F32)

    def allowed(k0, q_off, nk, nqq):
        kpos = k0 + lax.broadcasted_iota(jnp.int32, (nk, nqq), 0)
        qpos = q0 + q_off + lax.broadcasted_iota(jnp.int32, (nk, nqq), 1)
        if fox:
            return kpos <= qpos
        return jnp.right_shift(kpos, CHUNK_SHIFT) <= jnp.right_shift(qpos, CHUNK_SHIFT)

    def softmax_pv_diag(kt, s_ref):
        hk = tk // 2
        ks = pl.multiple_of(kt * tk, tk)
        ok_top = allowed(ks, 0, hk, tq)
        ok_bot = allowed(ks + hk, hk, hk, hk)
        for j in range(2):
            s_top = jnp.where(ok_top, s_ref[j, 0:hk, :], NEG)
            s_bot = jnp.where(ok_bot, s_ref[j, hk:tk, hk:tq], NEG)
            m_prev = m_ref[j]
            m_top = jnp.max(s_top, axis=0, keepdims=True)
            m_bot = jnp.max(s_bot, axis=0, keepdims=True)
            m_cur = jnp.concatenate([m_top[:, :hk], jnp.maximum(m_top[:, hk:], m_bot)], axis=1)
            m_new = jnp.maximum(m_prev, m_cur)
            p_top = jnp.exp2(s_top - m_new).astype(BF16)
            p_bot = jnp.exp2(s_bot - m_new[:, hk:]).astype(BF16)
            alpha = jnp.exp2(m_prev - m_new)
            rows = slice(j * VT_ROWS, (j + 1) * VT_ROWS)
            acc = alpha * acc_ref[j] + _dot(vt_ref[rows, pl.ds(ks, hk)], p_top)
            pv_bot = _dot(vt_ref[rows, pl.ds(pl.multiple_of(ks + hk, hk), hk)], p_bot)
            acc_ref[j] = jnp.concatenate([acc[:, :hk], acc[:, hk:] + pv_bot], axis=1)
            m_ref[j] = m_new

    def softmax_pv(kt, s_ref, masked):
        if masked and tq == tk and (tk // 2) % LANES == 0:
            softmax_pv_diag(kt, s_ref)
            return
        ks = pl.multiple_of(kt * tk, tk)
        if masked:
            kpos = ks + lax.broadcasted_iota(jnp.int32, (tk, tq), 0)
            qpos = q0 + lax.broadcasted_iota(jnp.int32, (tk, tq), 1)
            if fox:
                ok = kpos <= qpos
            else:
                ok = jnp.right_shift(kpos, CHUNK_SHIFT) <= jnp.right_shift(qpos, CHUNK_SHIFT)
        for j in range(2):
            s = s_ref[j]
            if masked:
                s = jnp.where(ok, s, NEG)
            m_prev = m_ref[j]
            m_new = jnp.maximum(m_prev, jnp.max(s, axis=0, keepdims=True))
            p = jnp.exp2(s - m_new)
            alpha = jnp.exp2(m_prev - m_new)
            vt = vt_ref[j * VT_ROWS:(j + 1) * VT_ROWS, pl.ds(ks, tk)]
            acc_ref[j] = alpha * acc_ref[j] + _dot(vt, p.astype(BF16))
            m_ref[j] = m_new

    def run(x_ref, y_ref, chain):
        def pair_body(c, carry):
            kt = 2 * c
            scores(kt + 1, y_ref, i)
            softmax_pv(kt, x_ref, False)
            scores(kt + 2, x_ref, i)
            softmax_pv(kt + 1, y_ref, False)
            return carry

        lax.fori_loop(0, nfull // 2, pair_body, 0)
        odd = (nfull % 2) == 1

        def tail(masked_ref, free_ref):
            if chain:
                inext = jnp.minimum(i + 1, nq - 1)
                scores(0, free_ref, inext, build_q(inext))
            softmax_pv(nfull, masked_ref, True)

        @pl.when(odd)
        def _():
            scores(nfull, y_ref, i)
            softmax_pv(nfull - 1, x_ref, False)
            tail(y_ref, x_ref)

        @pl.when(jnp.logical_not(odd))
        def _():
            tail(x_ref, y_ref)

    if nq == 1:
        scores(0, sa_ref, i, build_q(i))
        run(sa_ref, sb_ref, False)
    else:
        @pl.when(i == 0)
        def _():
            scores(0, sa_ref, i, build_q(i))

        in_a = (((i + 1) // 2) % 2) == 0

        @pl.when(in_a)
        def _():
            run(sa_ref, sb_ref, True)

        @pl.when(jnp.logical_not(in_a))
        def _():
            run(sb_ref, sa_ref, True)

    outs = []
    for j in range(2):
        acc = acc_ref[j]
        outs.append(acc[:V_DIM] * (1.0 / acc[V_DIM:V_DIM + 1]))
    o_ref[...] = jnp.concatenate(outs, axis=0).T


def _vt_from_rows(v, B, S):
    vt = v.reshape(B, S, H_MLA, V_DIM).transpose(0, 2, 3, 1)
    vt = jnp.concatenate([vt, jnp.ones((B, H_MLA, 1, S), BF16),
                          jnp.zeros((B, H_MLA, VT_ROWS - V_DIM - 1, S), BF16)], axis=2)
    return vt.reshape(B * H_MLA * VT_ROWS, S)


def _attention(qs, ks, vt, *, fox, B, T, S, tq, tk, p0):
    H2 = H_MLA // 2
    T_in = T
    if T < LANES:
        qs = tuple(jnp.pad(q.reshape(B, T, -1), ((0, 0), (0, LANES - T), (0, 0))).reshape(B * LANES, -1)
                   for q in qs)
        T = tq = LANES
    assert tk % tq == 0 and p0 % tk == 0 and S % tk == 0 and S >= p0 + T, (tq, tk, p0, S, T)
    nq = T // tq
    assert nq == 1 or (tq == tk and p0 == 0), (tq, tk, T, p0)
    qw = LANES if fox else 2 * HEAD_PAD
    qspec = pl.BlockSpec((T, qw), lambda b, h, i: (b, h))
    kspec = pl.BlockSpec((S, qw), lambda b, h, i: (b, h))
    vspec = pl.BlockSpec((2 * VT_ROWS, S), lambda b, h, i: (b * H2 + h, 0))
    if fox:
        in_specs = [qspec, qspec, kspec, kspec, vspec]
        args = (qs[0], qs[1], ks[0], ks[1], vt)
        scratch = [pltpu.VMEM((2, tq, 2 * LANES), BF16)]
    else:
        in_specs = [qspec, kspec, vspec]
        args = (qs[0], ks[0], vt)
        scratch = []
    scratch += [pltpu.VMEM((2, tk, tq), F32), pltpu.VMEM((2, tk, tq), F32),
                pltpu.VMEM((2, 1, tq), F32), pltpu.VMEM((2, VT_ROWS, tq), F32)]
    out = pl.pallas_call(
        functools.partial(_attn_kernel, fox=fox, tq=tq, tk=tk, p0=p0, nq=nq),
        out_shape=jax.ShapeDtypeStruct((B * T, H_MLA * V_DIM), F32),
        grid=(B, H2, nq),
        in_specs=in_specs,
        out_specs=pl.BlockSpec((tq, LANES), lambda b, h, i: (b * nq + i, h)),
        scratch_shapes=scratch,
        compiler_params=_cparams(("arbitrary", "arbitrary", "arbitrary")),
        name="fox_attention" if fox else "mla_attention",
    )(*args)
    if T_in != T:
        out = out.reshape(B, T, -1)[:, :T_in].reshape(B * T_in, -1)
    return out


def _post_kernel(om_ref, of_ref, x_ref, mod_ref, onm_ref, onf_ref, wo_ref, nf_ref, wrh_ref, wrl_ref, br_ref,
                 x1_ref, hg_ref):
    o = jnp.concatenate([_rms(om_ref[...], onm_ref[...]), _rms(of_ref[...], onf_ref[...])], axis=1)
    mix = _dot(o.astype(BF16), wo_ref[...])
    gate1 = mod_ref[0, 2:3, :]
    shift2 = mod_ref[0, 3:4, :]
    scale2 = mod_ref[0, 4:5, :]
    x1 = x_ref[...] + gate1 * mix
    x1_ref[...] = x1
    h2 = _rms(x1, nf_ref[...]) * (1.0 + scale2) + shift2
    hh = h2.astype(BF16)
    hg_ref[:, :h2.shape[1]] = hh.astype(F32)
    hl = (h2 - hh.astype(F32)).astype(BF16)
    wrh = wrh_ref[...]
    logits = _dot(hh, wrh) + _dot(hl, wrh) + _dot(hh, wrl_ref[...]) + br_ref[...]

    lane = lax.broadcasted_iota(jnp.int32, logits.shape, 1).astype(F32)
    big = float(LANES)
    gmask = lane < N_GROUPS
    gl = jnp.where(gmask, logits, NEG)
    gmax = jnp.max(gl, axis=1, keepdims=True)
    gsum = jnp.sum(jnp.where(gmask, jnp.exp(gl - gmax), 0.0), axis=1, keepdims=True)
    g_p = 1.0 / gsum
    g_idx = jnp.min(jnp.where(gl == gmax, lane, big), axis=1, keepdims=True)
    lo = ROUTER_OFF + g_idx * EXPERTS_PER_GROUP
    emask = (lane >= lo) & (lane < lo + EXPERTS_PER_GROUP)
    el = jnp.where(emask, logits, NEG)
    emax = jnp.max(el, axis=1, keepdims=True)
    ee = jnp.where(emask, jnp.exp(el - emax), 0.0)
    ep = ee / jnp.sum(ee, axis=1, keepdims=True)
    p1 = jnp.max(ep, axis=1, keepdims=True)
    i1 = jnp.min(jnp.where(emask & (ep == p1), lane, big), axis=1, keepdims=True)
    rest = emask & (lane != i1)
    ep2 = jnp.where(rest, ep, -1.0)
    p2 = jnp.max(ep2, axis=1, keepdims=True)
    i2 = jnp.min(jnp.where(rest & (ep2 == p2), lane, big), axis=1, keepdims=True)
    den = p1 + p2
    gate = (jnp.where(lane == i1, g_p * (p1 / den), 0.0) + jnp.where(lane == i2, g_p * (p2 / den), 0.0))
    g8 = jnp.zeros_like(gate)
    for g in range(N_GROUPS):
        g8 = g8 + pltpu.roll(gate, LANES - (ROUTER_OFF + g * EXPERTS_PER_GROUP), axis=1)
    d = h2.shape[1]
    hg_ref[:, d:] = jnp.where(lane < EXPERTS_PER_GROUP, g8, jnp.where(lane == GIDX_LANE, g_idx, 0.0))


def _post_attention(om, of, x, mod8, lw, tm):
    B, T, D = x.shape
    nt = T // tm
    N = B * T
    row = lambda w: pl.BlockSpec((tm, w), lambda b, i: (b * nt + i, 0))
    full = lambda a: pl.BlockSpec(a.shape, lambda b, i: (0,) * a.ndim)
    weights = [lw["onm"], lw["onf"], lw["wo"], lw["nf"], lw["wrh"], lw["wrl"], lw["br"]]
    HV = H_MLA * V_DIM
    return pl.pallas_call(
        _post_kernel,
        out_shape=[jax.ShapeDtypeStruct((N, D), F32), jax.ShapeDtypeStruct((N, D + LANES), F32)],
        grid=(B, nt),
        in_specs=[row(HV), row(HV), row(D), pl.BlockSpec((1, 8, D), lambda b, i: (b, 0, 0))]
                 + [full(w) for w in weights],
        out_specs=[row(D), row(D + LANES)],
        compiler_params=_cparams(("arbitrary", "arbitrary")),
        name="post_attention",
    )(om, of, x.reshape(N, D), mod8, *weights)


def _row_copies(n, make, start):
    def body(r, c):
        cp = make(r)
        cp.start() if start else cp.wait()
        return c
    lax.fori_loop(0, n, body, 0, unroll=8)


def _dispatch_kernel(pos_ref, h_ref, hs_in, hs_out, buf, sem, *, tm):
    del hs_in
    t = pl.program_id(0)
    nt = pl.num_programs(0)
    slot = t % 2

    def copy(r, s, start):
        dst = pos_ref[0, 0, r] if start else 0
        return pltpu.make_async_copy(buf.at[s, pl.ds(r, 1), :], hs_out.at[pl.ds(dst, 1), :], sem.at[s])

    buf[slot] = h_ref[...]
    _row_copies(tm, lambda r: copy(r, slot, True), True)

    @pl.when(t >= 1)
    def _():
        _row_copies(tm, lambda r: copy(r, 1 - slot, False), False)

    @pl.when(t == nt - 1)
    def _():
        _row_copies(tm, lambda r: copy(r, slot, False), False)


def _dispatch(hg, pos3, R, tm):
    N, W = hg.shape
    return pl.pallas_call(
        functools.partial(_dispatch_kernel, tm=tm),
        out_shape=jax.ShapeDtypeStruct((R, W), F32),
        grid=(N // tm,),
        in_specs=[pl.BlockSpec((1, 1, tm), lambda t: (t, 0, 0), memory_space=pltpu.SMEM),
                  pl.BlockSpec((tm, W), lambda t: (t, 0)), pl.BlockSpec(memory_space=pl.ANY)],
        out_specs=pl.BlockSpec(memory_space=pl.ANY),
        scratch_shapes=[pltpu.VMEM((2, tm, W), F32), pltpu.SemaphoreType.DMA((2,))],
        input_output_aliases={2: 0},
        compiler_params=_cparams(("arbitrary",)),
        name="moe_dispatch",
    )(pos3, hg, jnp.zeros((R, W), F32))


def _moe_kernel(tg_ref, hs_ref, wg_ref, wu_ref, wd_ref, ys_ref, *, d):
    del tg_ref
    rows = hs_ref[...]
    h = rows[:, :d].astype(BF16)
    g8 = rows[:, d:]
    lane = lax.broadcasted_iota(jnp.int32, g8.shape, 1)
    acc = jnp.zeros((rows.shape[0], d), F32)
    for j in range(EXPERTS_PER_GROUP):
        a = _dot(h, wg_ref[j])
        u = _dot(h, wu_ref[j])
        gcol = jnp.sum(jnp.where(lane == j, g8, 0.0), axis=1, keepdims=True)
        act = (a * (1.0 / (1.0 + jnp.exp(-a)))) * u * gcol
        acc = acc + _dot(act.astype(BF16), wd_ref[j])
    ys_ref[...] = acc


def _moe(hg, lw, tm, tmd):
    N, W = hg.shape
    d = W - LANES
    R = N + N_GROUPS * tm
    nt = R // tm
    grp = hg[:, d + GIDX_LANE].astype(jnp.int32)
    onehot = (grp[:, None] == jnp.arange(N_GROUPS, dtype=jnp.int32)[None, :]).astype(jnp.int32)
    counts = jnp.sum(onehot, axis=0)
    padded = ((counts + tm - 1) // tm) * tm
    ends = jnp.cumsum(padded)
    starts = ends - padded
    rank = jnp.sum((jnp.cumsum(onehot, axis=0) - 1) * onehot, axis=1)
    pos = jnp.sum(onehot * starts[None, :], axis=1) + rank
    tile_first = jnp.arange(nt, dtype=jnp.int32) * tm
    tile_group = jnp.minimum(jnp.sum((tile_first[:, None] >= ends[None, :]).astype(jnp.int32), axis=1),
                             N_GROUPS - 1)

    hs = _dispatch(hg, pos.reshape(N // tmd, 1, tmd), R, tmd)
    wspec = lambda a: pl.BlockSpec((EXPERTS_PER_GROUP,) + a.shape[1:], lambda t, tg: (tg[t], 0, 0))
    grid_spec = pltpu.PrefetchScalarGridSpec(
        num_scalar_prefetch=1,
        grid=(nt,),
        in_specs=[pl.BlockSpec((tm, W), lambda t, tg: (t, 0)),
                  wspec(lw["wg"]), wspec(lw["wu"]), wspec(lw["wd"])],
        out_specs=pl.BlockSpec((tm, d), lambda t, tg: (t, 0)),
    )
    ys = pl.pallas_call(
        functools.partial(_moe_kernel, d=d),
        out_shape=jax.ShapeDtypeStruct((R, d), F32),
        grid_spec=grid_spec,
        compiler_params=_cparams(("arbitrary",)),
        name="moe_experts",
    )(tile_group, hs, lw["wg"], lw["wu"], lw["wd"])
    return ys, pos


def _residual_kernel(pos_ref, nxt_ref, x_ref, mod_ref, ys_hbm, o_ref, ybuf, sem, *, tm):
    nt = pl.num_programs(1)
    step = pl.program_id(0) * nt + pl.program_id(1)
    total = pl.num_programs(0) * nt
    slot = step % 2

    def copy(idx_ref, r, s, start):
        src = idx_ref[0, 0, r] if start else 0
        return pltpu.make_async_copy(ys_hbm.at[pl.ds(src, 1), :], ybuf.at[s, pl.ds(r, 1), :], sem.at[s])

    @pl.when(step == 0)
    def _():
        _row_copies(tm, lambda r: copy(pos_ref, r, 0, True), True)

    @pl.when(step + 1 < total)
    def _():
        _row_copies(tm, lambda r: copy(nxt_ref, r, 1 - slot, True), True)

    _row_copies(tm, lambda r: copy(pos_ref, r, slot, False), False)
    o_ref[...] = x_ref[...] + mod_ref[0, 5:6, :] * ybuf[slot]


def _residual(x1, ys, pos, mod8, tm):
    B = mod8.shape[0]
    N, D = x1.shape
    nt = N // B // tm
    ns = N // tm
    pos3 = pos.reshape(ns, 1, tm)
    row = pl.BlockSpec((tm, D), lambda b, i: (b * nt + i, 0))
    idx = lambda f: pl.BlockSpec((1, 1, tm), f, memory_space=pltpu.SMEM)
    return pl.pallas_call(
        functools.partial(_residual_kernel, tm=tm),
        out_shape=jax.ShapeDtypeStruct((N, D), F32),
        grid=(B, nt),
        in_specs=[idx(lambda b, i: (b * nt + i, 0, 0)),
                  idx(lambda b, i: (jnp.minimum(b * nt + i + 1, ns - 1), 0, 0)),
                  row, pl.BlockSpec((1, 8, D), lambda b, i: (b, 0, 0)), pl.BlockSpec(memory_space=pl.ANY)],
        out_specs=row,
        scratch_shapes=[pltpu.VMEM((2, tm, D), F32), pltpu.SemaphoreType.DMA((2,))],
        compiler_params=_cparams(("arbitrary", "arbitrary")),
        name="ffn_residual",
    )(pos3, pos3, x1, mod8, ys)


def _norm_kernel(x_ref, g_ref, o_ref):
    o_ref[...] = _rms(x_ref[...], g_ref[...])


def _final_norm(x, g, tm):
    B, T, D = x.shape
    N = B * T
    return pl.pallas_call(
        _norm_kernel,
        out_shape=jax.ShapeDtypeStruct((N, D), F32),
        grid=(N // tm,),
        in_specs=[pl.BlockSpec((tm, D), lambda i: (i, 0)), pl.BlockSpec((1, D), lambda i: (0, 0))],
        out_specs=pl.BlockSpec((tm, D), lambda i: (i, 0)),
        compiler_params=_cparams(("arbitrary",)),
        name="final_norm",
    )(x.reshape(N, D), g.reshape(1, D)).reshape(B, T, D)


def _rope_tables(p0, T):
    half = ROPE_DIM // 2
    inv = ROPE_THETA ** (-jnp.arange(half, dtype=F32) / half)
    ang = (p0 + jnp.arange(T, dtype=jnp.int32)).astype(F32)[:, None] * inv[None, :]
    cos, sin = jnp.cos(ang), jnp.sin(ang)
    z = lambda w: jnp.zeros((T, w), F32)
    one = jnp.ones((T, NOPE_DIM), F32)
    tail = z(HEAD_PAD - NOPE_DIM - ROPE_DIM)
    qs = MLA_SCALE * LOG2E
    cq = jnp.concatenate([one, cos, cos, tail], axis=1) * qs
    s1q = jnp.concatenate([z(NOPE_DIM), -sin, z(half), tail], axis=1) * qs
    s2q = jnp.concatenate([z(NOPE_DIM), z(half), sin, tail], axis=1) * qs
    kt = z(LANES - ROPE_DIM)
    ck = jnp.concatenate([cos, cos, kt], axis=1)
    s1k = jnp.concatenate([-sin, z(half), kt], axis=1)
    s2k = jnp.concatenate([z(half), sin, kt], axis=1)
    return cq, s1q, s2q, ck, s1k, s2k


def _layer_weights(l, p):
    D = p["w_in"].shape[1]
    w_in = p["w_in"][l]
    o = np.cumsum([0, Q_LORA, KV_LORA, ROPE_DIM, H_FOX * FOX_DIM, H_FOX * FOX_DIM, H_FOX * FOX_DIM, H_FOX])
    seg = lambda i: w_in[:, o[i]:o[i + 1]]
    padl = lambda w: jnp.pad(w, ((0, 0), (0, LANES - w.shape[1])))
    r1 = lambda v: v.reshape(1, -1).astype(F32)
    wqu = p["w_q_up"][l].reshape(Q_LORA, H_MLA, NOPE_DIM + ROPE_DIM)
    wqu = jnp.pad(wqu, ((0, 0), (0, 0), (0, HEAD_PAD - NOPE_DIM - ROPE_DIM))).reshape(Q_LORA, H_MLA * HEAD_PAD)
    wkv = p["w_kv_up"][l].reshape(KV_LORA, H_MLA, NOPE_DIM + V_DIM)
    wk_c = jnp.pad(wkv[:, :, :NOPE_DIM], ((0, 0), (0, 0), (0, HEAD_PAD - NOPE_DIM)))
    place = np.zeros((LANES, H_MLA, HEAD_PAD), np.float32)
    for j in range(ROPE_DIM):
        place[j, :, NOPE_DIM + j] = 1.0
    wk = jnp.concatenate([wk_c.reshape(KV_LORA, H_MLA * HEAD_PAD),
                          jnp.asarray(place.reshape(LANES, H_MLA * HEAD_PAD))], axis=0)
    wv = wkv[:, :, NOPE_DIM:].reshape(KV_LORA, H_MLA * V_DIM)
    wr = padl(jnp.concatenate([p["w_router_group"][l], p["w_router_expert"][l]], axis=1))
    wrh = wr.astype(BF16)
    wrl = (wr - wrh.astype(F32)).astype(BF16)
    br = padl(jnp.concatenate([p["b_router_group"][l], p["b_router_expert"][l]]).reshape(1, -1)).astype(F32)
    return dict(
        na=r1(p["norm_attn"][l]), wq=seg(0).astype(BF16), wkv=seg(1).astype(BF16), wkr=padl(seg(2)).astype(BF16),
        wfq=seg(3).astype(BF16), wfk=seg(4).astype(BF16), wfv=seg(5).astype(BF16), wfl=padl(seg(6)).astype(BF16),
        bfl=padl(r1(p["b_forget"][l])), qn=r1(p["q_norm"][l]), wqu=wqu.astype(BF16), kvn=r1(p["kv_norm"][l]),
        wk=wk.astype(BF16), wvt=wv.T.astype(BF16), wfvt=seg(5).T.astype(BF16),
        onm=r1(p["out_norm_mla"][l]), onf=r1(p["out_norm_fox"][l]), wo=p["w_out"][l].astype(BF16),
        nf=r1(p["norm_ffn"][l]), wrh=wrh, wrl=wrl, br=br,
        wg=p["w_gate"][l].astype(BF16), wu=p["w_up"][l].astype(BF16), wd=p["w_down"][l].astype(BF16),
    )


def _tile(n, pref):
    return pref if n % pref == 0 else n


def _layer(x, mod8, lw, past, cfg):
    B, T, D = x.shape
    tm = cfg["tm"]
    aug_consts = (cfg["tril3"], cfg["pq"], cfg["pk"])
    if past is None:
        P = 0
        cinit = jnp.zeros((B, 8, LANES), F32)
    else:
        P = past[0].shape[1]
        augk_past, cinit = _past_forget(past[4].astype(F32), (cfg["tril3_past"], cfg["pq"], cfg["pk"]),
                                        cfg["tm_past"])
    (ckv, kro, krp, kf, vf, lf, qm, qf, kfb, vfb, aq, ak, vft) = _pre_attention(
        x, mod8, lw, cfg["tabs"], aug_consts, cinit, tm)
    new_rows = (ckv.reshape(B, T, KV_LORA), kro.reshape(B, T, ROPE_DIM),
                kf.reshape(B, T, H_FOX, FOX_DIM), vf.reshape(B, T, H_FOX, FOX_DIM), lf.reshape(B, T, H_FOX))

    S = P + T
    if past is None:
        ckv_all, krp_all, kfb_all, ak_all = ckv, krp, kfb, ak
        Sp = S
    else:
        Sp = -(-S // cfg["tk"]) * cfg["tk"]
        HD = H_FOX * FOX_DIM

        def cat(old, new, w):
            a = jnp.concatenate([old.reshape(B, P, w).astype(new.dtype), new.reshape(B, T, w)], axis=1)
            return jnp.pad(a, ((0, 0), (0, Sp - S), (0, 0))).reshape(B * Sp, w)

        ckv_all = cat(past[0], ckv, KV_LORA)
        krp_all = cat(jnp.pad(past[1], ((0, 0), (0, 0), (0, LANES - ROPE_DIM))), krp, LANES)
        kfb_all = cat(past[2], kfb, HD)
        vft = _vt_from_rows(cat(past[3], vfb, HD), B, Sp)
        ak_all = cat(augk_past, ak, HD)
    km, vmt = _kv_up(ckv_all, krp_all, lw["wk"], lw["wvt"], B, Sp, _tile(Sp, 512))
    att = dict(B=B, T=T, S=Sp, tq=cfg["tq"], tk=cfg["tk"], p0=P)
    om = _attention((qm,), (km,), vmt, fox=False, **att)
    of = _attention((qf, aq), (kfb_all, ak_all), vft, fox=True, **att)
    x1, hg = _post_attention(om, of, x, mod8, lw, tm)
    ys, pos = _moe(hg, lw, cfg["tms"], cfg["tms"])
    x2 = _residual(x1, ys, pos, mod8, tm).reshape(B, T, D)
    return x2, new_rows


def _group_cfg(B, T, P, pq, pk):
    tm = _tile(T, 256)
    tq = _tile(T, 512)
    tk = tq if P == 0 else (512 if P % 512 == 0 else LANES)
    cfg = dict(tm=tm, tq=tq, tk=tk, tms=_tile(B * T, 512), tril3=_tril3(tm), pq=pq, pk=pk,
               tabs=_rope_tables(P, T))
    if P:
        cfg["tm_past"] = _tile(P, 256)
        cfg["tril3_past"] = _tril3(cfg["tm_past"])
    return cfg


def kernel(x_prompt, x_sample, cache_mla_ckv, cache_mla_krope, cache_fox_k, cache_fox_v, cache_fox_logf, c_prompt, c_sample, w_mod, b_mod, norm_attn, w_in, b_forget, q_norm, w_q_up, kv_norm, w_kv_up, out_norm_mla, out_norm_fox, w_out, norm_ffn, w_router_group, b_router_group, w_router_expert, b_router_expert, w_gate, w_up, w_down, final_norm):
    params = dict(norm_attn=norm_attn, w_in=w_in, b_forget=b_forget, q_norm=q_norm, w_q_up=w_q_up,
                  kv_norm=kv_norm, w_kv_up=w_kv_up, out_norm_mla=out_norm_mla, out_norm_fox=out_norm_fox,
                  w_out=w_out, norm_ffn=norm_ffn, w_router_group=w_router_group, b_router_group=b_router_group,
                  w_router_expert=w_router_expert, b_router_expert=b_router_expert,
                  w_gate=w_gate, w_up=w_up, w_down=w_down)
    L = w_mod.shape[0]
    Bp, Tp, D = x_prompt.shape
    Bs, Ts, _ = x_sample.shape
    P = cache_mla_ckv.shape[2]

    R = -(-(Bp + Bs) // 8) * 8
    c_all = jnp.pad(jnp.concatenate([c_prompt, c_sample], axis=0), ((0, R - Bp - Bs), (0, 0)))
    mod = _mod_all(c_all, w_mod, b_mod).reshape(L, R, 6, D)
    mod = jnp.pad(mod, ((0, 0), (0, 0), (0, 2), (0, 0)))

    pq, pk = _aug_placement()
    cfg_p = _group_cfg(Bp, Tp, 0, pq, pk)
    cfg_s = _group_cfg(Bs, Ts, P, pq, pk)

    y_p, y_s = x_prompt, x_sample
    rows_p, rows_s = [], []
    for l in range(L):
        lw = _layer_weights(l, params)
        y_p, r_p = _layer(y_p, mod[l, :Bp], lw, None, cfg_p)
        past = (cache_mla_ckv[l], cache_mla_krope[l], cache_fox_k[l], cache_fox_v[l], cache_fox_logf[l])
        y_s, r_s = _layer(y_s, mod[l, Bp:Bp + Bs], lw, past, cfg_s)
        rows_p.append(r_p)
        rows_s.append(r_s)

    y_prompt = _final_norm(y_p, final_norm, _tile(Bp * Tp, 512))
    y_sample = _final_norm(y_s, final_norm, _tile(Bs * Ts, 512))
    stack = lambda rows, k: jnp.stack([r[k] for r in rows])
    return (y_prompt, y_sample,
            stack(rows_p, 0), stack(rows_p, 1), stack(rows_p, 2), stack(rows_p, 3), stack(rows_p, 4),
            stack(rows_s, 0), stack(rows_s, 1), stack(rows_s, 2), stack(rows_s, 3), stack(rows_s, 4))
```

```python
import functools

import numpy as np
import jax
import jax.numpy as jnp
from jax import lax
from jax.experimental import pallas as pl
from jax.experimental.pallas import tpu as pltpu

CHUNK = 64
CHUNK_SHIFT = 6
EPS = 1e-6
H_MLA = 8
NOPE_DIM = 64
ROPE_DIM = 32
V_DIM = 64
Q_LORA = 384
KV_LORA = 256
ROPE_THETA = 10000.0
MLA_SCALE = (NOPE_DIM + ROPE_DIM) ** -0.5
H_FOX = 8
FOX_DIM = 64
FOX_SCALE = FOX_DIM ** -0.5
N_GROUPS = 4
EXPERTS_PER_GROUP = 8
N_EXPERTS = N_GROUPS * EXPERTS_PER_GROUP
D_EXPERT = 256

LANES = 128
HEAD_PAD = LANES
VT_ROWS = 80
ROUTER_OFF = N_GROUPS
GIDX_LANE = 16
NEG = -1e30
LOG2E = 1.4426950408889634
VMEM_LIMIT = 56 * 1024 * 1024

F32 = jnp.float32
BF16 = jnp.bfloat16


def _cparams(sem):
    return pltpu.CompilerParams(dimension_semantics=sem, vmem_limit_bytes=VMEM_LIMIT)


def _rms(x, g):
    return x * lax.rsqrt(jnp.mean(x * x, axis=-1, keepdims=True) + EPS) * g


def _dot(a, b):
    return jnp.dot(a, b, preferred_element_type=F32)


def _split3(x):
    hi = x.astype(BF16).astype(F32)
    r = x - hi
    mid = r.astype(BF16).astype(F32)
    lo = (r - mid).astype(BF16).astype(F32)
    return hi, mid, lo


def _vt_rows(vt):
    n = vt.shape[1]
    r = lax.broadcasted_iota(jnp.int32, (VT_ROWS - V_DIM, n), 0)
    tail = jnp.where(r == 0, 1.0, 0.0).astype(BF16)
    vb = vt.astype(BF16)
    parts = []
    for h in range(H_MLA):
        parts += [vb[h * V_DIM:(h + 1) * V_DIM], tail]
    return jnp.concatenate(parts, axis=0)


def _nt_dot(a, b):
    return lax.dot_general(a, b, (((1,), (1,)), ((), ())), preferred_element_type=F32)


def _mod_kernel(c_ref, w_ref, b_ref, o_ref):
    o_ref[0] = _dot(c_ref[...].astype(BF16), w_ref[0].astype(BF16)) + b_ref[0]


def _mod_all(c_all, w_mod, b_mod):
    L, D, N6 = w_mod.shape
    R = c_all.shape[0]
    TN = N6 // 4
    return pl.pallas_call(
        _mod_kernel,
        out_shape=jax.ShapeDtypeStruct((L, R, N6), F32),
        grid=(L, N6 // TN),
        in_specs=[pl.BlockSpec((R, D), lambda l, j: (0, 0)),
                  pl.BlockSpec((1, D, TN), lambda l, j: (l, 0, j)),
                  pl.BlockSpec((1, 1, TN), lambda l, j: (l, 0, j))],
        out_specs=pl.BlockSpec((1, R, TN), lambda l, j: (l, 0, j)),
        compiler_params=_cparams(("arbitrary", "arbitrary")),
        name="adaln_mod",
    )(c_all, w_mod, b_mod.reshape(L, 1, N6))


def _fox_cum_aug(logf, carry_row, tril3_ref, pq_ref, pk_ref):
    hi, mid, lo = _split3(logf)
    stacked = jnp.concatenate([hi.astype(BF16), mid.astype(BF16), lo.astype(BF16)], axis=0)
    cum = _dot(tril3_ref[...], stacked) + carry_row
    lane = lax.broadcasted_iota(jnp.int32, cum.shape, 1)
    cum = jnp.where(lane < H_FOX, cum, 0.0)
    fh, fm, fl = _split3(cum * LOG2E)
    a3 = (fh + pltpu.roll(fm, H_FOX, axis=1) + pltpu.roll(fl, 2 * H_FOX, axis=1)
          + jnp.where(lane == 3 * H_FOX, 1.0, 0.0)).astype(BF16)
    augq = _dot(a3, pq_ref[...]).astype(BF16)
    augk = _dot(a3, pk_ref[...]).astype(BF16)
    return cum, augq, augk


def _aug_placement():
    pq = np.zeros((LANES, H_FOX * FOX_DIM), np.float32)
    pk = np.zeros((LANES, H_FOX * FOX_DIM), np.float32)
    for h in range(H_FOX):
        base = h * FOX_DIM
        for j in range(3):
            pq[3 * H_FOX, base + j] = 1.0
            pq[j * H_FOX + h, base + 3 + j] = 1.0
            pk[j * H_FOX + h, base + j] = -1.0
            pk[3 * H_FOX, base + 3 + j] = 1.0
    return jnp.asarray(pq, BF16), jnp.asarray(pk, BF16)


def _tril3(tm):
    t = np.tril(np.ones((tm, tm), np.float32))
    return jnp.asarray(np.concatenate([t, t, t], axis=1), BF16)


def _rope_lanes(x, c, s1, s2):
    return x * c + pltpu.roll(x, LANES - ROPE_DIM // 2, axis=1) * s1 + pltpu.roll(x, ROPE_DIM // 2, axis=1) * s2


def _pre_kernel(x_ref, mod_ref, na_ref, wq_ref, wkv_ref, wkr_ref, wfq_ref, wfk_ref, wfv_ref, wfl_ref, bfl_ref,
                qn_ref, wqu_ref, kvn_ref, wfvt_ref, cq_ref, s1q_ref, s2q_ref, ck_ref, s1k_ref, s2k_ref,
                tril3_ref, pq_ref, pk_ref, cinit_ref,
                ckv_ref, kro_ref, krp_ref, kf_ref, vf_ref, lf_ref, qm_ref, qf_ref, kfb_ref, vfb_ref,
                aq_ref, ak_ref, vft_ref, carry_ref):
    @pl.when(pl.program_id(1) == 0)
    def _():
        carry_ref[...] = cinit_ref[0]

    x = x_ref[...]
    shift = mod_ref[0, 0:1, :]
    scale = mod_ref[0, 1:2, :]
    h = _rms(x, na_ref[...]) * (1.0 + scale) + shift
    hb = h.astype(BF16)

    qn = _rms(_dot(hb, wq_ref[...]), qn_ref[...]).astype(BF16)
    q = _dot(qn, wqu_ref[...])
    cq, s1q, s2q = cq_ref[...], s1q_ref[...], s2q_ref[...]
    for hh in range(H_MLA):
        sl = slice(hh * HEAD_PAD, (hh + 1) * HEAD_PAD)
        qm_ref[:, sl] = _rope_lanes(q[:, sl], cq, s1q, s2q).astype(BF16)

    ckv_ref[...] = _rms(_dot(hb, wkv_ref[...]), kvn_ref[...])
    kr = _rope_lanes(_dot(hb, wkr_ref[...]), ck_ref[...], s1k_ref[...], s2k_ref[...])
    kro_ref[...] = kr[:, :ROPE_DIM]
    krp_ref[...] = kr.astype(BF16)

    qf_ref[...] = (_dot(hb, wfq_ref[...]) * (FOX_SCALE * LOG2E)).astype(BF16)
    kf = _dot(hb, wfk_ref[...])
    kf_ref[...] = kf
    kfb_ref[...] = kf.astype(BF16)
    vf = _dot(hb, wfv_ref[...])
    vf_ref[...] = vf
    vfb_ref[...] = vf.astype(BF16)
    vft_ref[...] = _vt_rows(_nt_dot(wfvt_ref[...], hb))
    z = _dot(hb, wfl_ref[...]) + bfl_ref[...]
    logf = jnp.minimum(z, 0.0) - jnp.log1p(jnp.exp(-jnp.abs(z)))
    lane = lax.broadcasted_iota(jnp.int32, logf.shape, 1)
    logf = jnp.where(lane < H_FOX, logf, 0.0)
    lf_ref[...] = logf[:, :H_FOX]
    cum, augq, augk = _fox_cum_aug(logf, carry_ref[0:1, :], tril3_ref, pq_ref, pk_ref)
    tm = cum.shape[0]
    carry_ref[...] = jnp.broadcast_to(cum[tm - 1:tm, :], carry_ref.shape)
    aq_ref[...] = augq
    ak_ref[...] = augk


def _pre_attention(x, mod8, lw, tabs, consts, cinit, tm):
    B, T, D = x.shape
    nt = T // tm
    N = B * T
    xf = x.reshape(N, D)
    row = lambda w: pl.BlockSpec((tm, w), lambda b, i: (b * nt + i, 0))
    full = lambda a: pl.BlockSpec(a.shape, lambda b, i: (0,) * a.ndim)
    tab = pl.BlockSpec((tm, LANES), lambda b, i: (i, 0))
    weights = [lw["na"], lw["wq"], lw["wkv"], lw["wkr"], lw["wfq"], lw["wfk"], lw["wfv"], lw["wfl"], lw["bfl"],
               lw["qn"], lw["wqu"], lw["kvn"], lw["wfvt"]]
    HD = H_FOX * FOX_DIM
    outs = [(KV_LORA, F32), (ROPE_DIM, F32), (LANES, BF16), (HD, F32), (HD, F32), (H_FOX, F32),
            (H_MLA * HEAD_PAD, BF16), (HD, BF16), (HD, BF16), (HD, BF16), (HD, BF16), (HD, BF16)]
    VR = H_FOX * VT_ROWS
    return pl.pallas_call(
        _pre_kernel,
        out_shape=[jax.ShapeDtypeStruct((N, w), dt) for w, dt in outs] + [jax.ShapeDtypeStruct((B * VR, T), BF16)],
        grid=(B, nt),
        in_specs=([row(D), pl.BlockSpec((1, 8, D), lambda b, i: (b, 0, 0))] + [full(w) for w in weights]
                  + [tab] * 6 + [full(c) for c in consts]
                  + [pl.BlockSpec((1, 8, LANES), lambda b, i: (b, 0, 0))]),
        out_specs=[row(w) for w, _ in outs] + [pl.BlockSpec((VR, tm), lambda b, i: (b, i))],
        scratch_shapes=[pltpu.VMEM((8, LANES), F32)],
        compiler_params=_cparams(("arbitrary", "arbitrary")),
        name="pre_attention",
    )(xf, mod8, *weights, *tabs, *consts, cinit)


def _past_kernel(lf_ref, tril3_ref, pq_ref, pk_ref, ak_ref, last_ref, carry_ref):
    @pl.when(pl.program_id(1) == 0)
    def _():
        carry_ref[...] = jnp.zeros_like(carry_ref)

    cum, _, augk = _fox_cum_aug(lf_ref[...], carry_ref[0:1, :], tril3_ref, pq_ref, pk_ref)
    tm = cum.shape[0]
    carry_ref[...] = jnp.broadcast_to(cum[tm - 1:tm, :], carry_ref.shape)
    ak_ref[...] = augk
    last_ref[0] = carry_ref[...]


def _past_forget(logf_past, consts, tm):
    B, P, H = logf_past.shape
    nt = P // tm
    lf = jnp.pad(logf_past.reshape(B * P, H), ((0, 0), (0, LANES - H)))
    full = lambda a: pl.BlockSpec(a.shape, lambda b, i: (0,) * a.ndim)
    return pl.pallas_call(
        _past_kernel,
        out_shape=[jax.ShapeDtypeStruct((B * P, H_FOX * FOX_DIM), BF16),
                   jax.ShapeDtypeStruct((B, 8, LANES), F32)],
        grid=(B, nt),
        in_specs=[pl.BlockSpec((tm, LANES), lambda b, i: (b * nt + i, 0))] + [full(c) for c in consts],
        out_specs=[pl.BlockSpec((tm, H_FOX * FOX_DIM), lambda b, i: (b * nt + i, 0)),
                   pl.BlockSpec((1, 8, LANES), lambda b, i: (b, 0, 0))],
        scratch_shapes=[pltpu.VMEM((8, LANES), F32)],
        compiler_params=_cparams(("arbitrary", "arbitrary")),
        name="past_forget",
    )(lf, *consts)


def _kvup_kernel(ckv_ref, krp_ref, wk_ref, wvt_ref, k_ref, vt_ref):
    cb = ckv_ref[...].astype(BF16)
    k_ref[...] = _dot(jnp.concatenate([cb, krp_ref[...]], axis=1), wk_ref[...]).astype(BF16)
    vt_ref[...] = _vt_rows(_nt_dot(wvt_ref[...], cb))


def _kv_up(ckv, krp, wk, wvt, B, S, tm):
    nt = S // tm
    row = lambda w: pl.BlockSpec((tm, w), lambda b, i: (b * nt + i, 0))
    full = lambda a: pl.BlockSpec(a.shape, lambda b, i: (0,) * a.ndim)
    VR = H_MLA * VT_ROWS
    return pl.pallas_call(
        _kvup_kernel,
        out_shape=[jax.ShapeDtypeStruct((B * S, H_MLA * HEAD_PAD), BF16),
                   jax.ShapeDtypeStruct((B * VR, S), BF16)],
        grid=(B, nt),
        in_specs=[row(KV_LORA), row(LANES), full(wk), full(wvt)],
        out_specs=[row(H_MLA * HEAD_PAD), pl.BlockSpec((VR, tm), lambda b, i: (b, i))],
        compiler_params=_cparams(("arbitrary", "arbitrary")),
        name="mla_kv_up",
    )(ckv, krp, wk, wvt)


def _attn_kernel(*refs, fox, tq, tk, p0, nq):
    if fox:
        q_ref, aq_ref, k_ref, ak_ref, vt_ref, o_ref, qs_ref, sa_ref, sb_ref, m_ref, acc_ref = refs
    else:
        q_ref, k_ref, vt_ref, o_ref, sa_ref, sb_ref, m_ref, acc_ref = refs
    i = pl.program_id(2)
    q0 = p0 + i * tq
    nfull = q0 // tk

    def build_q(iq):
        if not fox:
            return None
        rs = pl.multiple_of(iq * tq, tq)
        lane = lax.broadcasted_iota(jnp.int32, (tq, LANES), 1)
        q = q_ref[pl.ds(rs, tq), :]
        aq = aq_ref[pl.ds(rs, tq), :]
        zero = jnp.zeros_like(q)
        vals = []
        for j in range(2):
            own = (lane >= j * FOX_DIM) & (lane < (j + 1) * FOX_DIM)
            vals.append(jnp.concatenate([jnp.where(own, q, zero), jnp.where(own, aq, zero)], axis=1))
            qs_ref[j] = vals[j]
        return vals

    m_ref[...] = jnp.full(m_ref.shape, NEG, F32)
    acc_ref[...] = jnp.zeros(acc_ref.shape, F32)

    def scores(kt, s_ref, iq, qvals=None):
        ks = pl.multiple_of(kt * tk, tk)
        rs = pl.multiple_of(iq * tq, tq)
        if fox:
            kj = jnp.concatenate([k_ref[pl.ds(ks, tk), :], ak_ref[pl.ds(ks, tk), :]], axis=1)
        for j in range(2):
            if fox:
                qj = qs_ref[j] if qvals is None else qvals[j]
            else:
                qj = q_ref[pl.ds(rs, tq), j * HEAD_PAD:(j + 1) * HEAD_PAD]
                kj = k_ref[pl.ds(ks, tk), j * HEAD_PAD:(j + 1) * HEAD_PAD]
            s_ref[j] = lax.dot_general(kj, qj, (((1,), (1,)), ((), ())), preferred_element_type=F32)

    def allowed(k0, q_off, nk, nqq):
        kpos = k0 + lax.broadcasted_iota(jnp.int32, (nk, nqq), 0)
        qpos = q0 + q_off + lax.broadcasted_iota(jnp.int32, (nk, nqq), 1)
        if fox:
            return kpos <= qpos
        return jnp.right_shift(kpos, CHUNK_SHIFT) <= jnp.right_shift(qpos, CHUNK_SHIFT)

    def softmax_pv_diag(kt, s_ref):
        hk = tk // 2
        ks = pl.multiple_of(kt * tk, tk)
        ok_top = allowed(ks, 0, hk, tq)
        ok_bot = allowed(ks + hk, hk, hk, hk)
        for j in range(2):
            s_top = jnp.where(ok_top, s_ref[j, 0:hk, :], NEG)
            s_bot = jnp.where(ok_bot, s_ref[j, hk:tk, hk:tq], NEG)
            m_prev = m_ref[j]
            m_top = jnp.max(s_top, axis=0, keepdims=True)
            m_bot = jnp.max(s_bot, axis=0, keepdims=True)
            m_cur = jnp.concatenate([m_top[:, :hk], jnp.maximum(m_top[:, hk:], m_bot)], axis=1)
            m_new = jnp.maximum(m_prev, m_cur)
            p_top = jnp.exp2(s_top - m_new).astype(BF16)
            p_bot = jnp.exp2(s_bot - m_new[:, hk:]).astype(BF16)
            alpha = jnp.exp2(m_prev - m_new)
            rows = slice(j * VT_ROWS, (j + 1) * VT_ROWS)
            acc = alpha * acc_ref[j] + _dot(vt_ref[rows, pl.ds(ks, hk)], p_top)
            pv_bot = _dot(vt_ref[rows, pl.ds(pl.multiple_of(ks + hk, hk), hk)], p_bot)
            acc_ref[j] = jnp.concatenate([acc[:, :hk], acc[:, hk:] + pv_bot], axis=1)
            m_ref[j] = m_new

    def softmax_pv(kt, s_ref, masked):
        if masked and tq == tk and (tk // 2) % LANES == 0:
            softmax_pv_diag(kt, s_ref)
            return
        ks = pl.multiple_of(kt * tk, tk)
        if masked:
            kpos = ks + lax.broadcasted_iota(jnp.int32, (tk, tq), 0)
            qpos = q0 + lax.broadcasted_iota(jnp.int32, (tk, tq), 1)
            if fox:
                ok = kpos <= qpos
            else:
                ok = jnp.right_shift(kpos, CHUNK_SHIFT) <= jnp.right_shift(qpos, CHUNK_SHIFT)
        for j in range(2):
            s = s_ref[j]
            if masked:
                s = jnp.where(ok, s, NEG)
            m_prev = m_ref[j]
            m_new = jnp.maximum(m_prev, jnp.max(s, axis=0, keepdims=True))
            p = jnp.exp2(s - m_new)
            alpha = jnp.exp2(m_prev - m_new)
            vt = vt_ref[j * VT_ROWS:(j + 1) * VT_ROWS, pl.ds(ks, tk)]
            acc_ref[j] = alpha * acc_ref[j] + _dot(vt, p.astype(BF16))
            m_ref[j] = m_new

    def run(x_ref, y_ref, chain):
        def pair_body(c, carry):
            kt = 2 * c
            scores(kt + 1, y_ref, i)
            softmax_pv(kt, x_ref, False)
            scores(kt + 2, x_ref, i)
            softmax_pv(kt + 1, y_ref, False)
            return carry

        lax.fori_loop(0, nfull // 2, pair_body, 0)
        odd = (nfull % 2) == 1

        def tail(masked_ref, free_ref):
            if chain:
                inext = jnp.minimum(i + 1, nq - 1)
                scores(0, free_ref, inext, build_q(inext))
            softmax_pv(nfull, masked_ref, True)

        @pl.when(odd)
        def _():
            scores(nfull, y_ref, i)
            softmax_pv(nfull - 1, x_ref, False)
            tail(y_ref, x_ref)

        @pl.when(jnp.logical_not(odd))
        def _():
            tail(x_ref, y_ref)

    if nq == 1:
        scores(0, sa_ref, i, build_q(i))
        run(sa_ref, sb_ref, False)
    else:
        @pl.when(i == 0)
        def _():
            scores(0, sa_ref, i, build_q(i))

        in_a = (((i + 1) // 2) % 2) == 0

        @pl.when(in_a)
        def _():
            run(sa_ref, sb_ref, True)

        @pl.when(jnp.logical_not(in_a))
        def _():
            run(sb_ref, sa_ref, True)

    outs = []
    for j in range(2):
        acc = acc_ref[j]
        outs.append(acc[:V_DIM] * (1.0 / acc[V_DIM:V_DIM + 1]))
    o_ref[...] = jnp.concatenate(outs, axis=0).T


def _vt_from_rows(v, B, S):
    vt = v.reshape(B, S, H_MLA, V_DIM).transpose(0, 2, 3, 1)
    vt = jnp.concatenate([vt, jnp.ones((B, H_MLA, 1, S), BF16),
                          jnp.zeros((B, H_MLA, VT_ROWS - V_DIM - 1, S), BF16)], axis=2)
    return vt.reshape(B * H_MLA * VT_ROWS, S)


def _attention(qs, ks, vt, *, fox, B, T, S, tq, tk, p0):
    H2 = H_MLA // 2
    T_in = T
    if T < LANES:
        qs = tuple(jnp.pad(q.reshape(B, T, -1), ((0, 0), (0, LANES - T), (0, 0))).reshape(B * LANES, -1)
                   for q in qs)
        T = tq = LANES
    assert tk % tq == 0 and p0 % tk == 0 and S % tk == 0 and S >= p0 + T, (tq, tk, p0, S, T)
    nq = T // tq
    assert nq == 1 or (tq == tk and p0 == 0), (tq, tk, T, p0)
    qw = LANES if fox else 2 * HEAD_PAD
    qspec = pl.BlockSpec((T, qw), lambda b, h, i: (b, h))
    kspec = pl.BlockSpec((S, qw), lambda b, h, i: (b, h))
    vspec = pl.BlockSpec((2 * VT_ROWS, S), lambda b, h, i: (b * H2 + h, 0))
    if fox:
        in_specs = [qspec, qspec, kspec, kspec, vspec]
        args = (qs[0], qs[1], ks[0], ks[1], vt)
        scratch = [pltpu.VMEM((2, tq, 2 * LANES), BF16)]
    else:
        in_specs = [qspec, kspec, vspec]
        args = (qs[0], ks[0], vt)
        scratch = []
    scratch += [pltpu.VMEM((2, tk, tq), F32), pltpu.VMEM((2, tk, tq), F32),
                pltpu.VMEM((2, 1, tq), F32), pltpu.VMEM((2, VT_ROWS, tq), F32)]
    out = pl.pallas_call(
        functools.partial(_attn_kernel, fox=fox, tq=tq, tk=tk, p0=p0, nq=nq),
        out_shape=jax.ShapeDtypeStruct((B * T, H_MLA * V_DIM), F32),
        grid=(B, H2, nq),
        in_specs=in_specs,
        out_specs=pl.BlockSpec((tq, LANES), lambda b, h, i: (b * nq + i, h)),
        scratch_shapes=scratch,
        compiler_params=_cparams(("arbitrary", "arbitrary", "arbitrary")),
        name="fox_attention" if fox else "mla_attention",
    )(*args)
    if T_in != T:
        out = out.reshape(B, T, -1)[:, :T_in].reshape(B * T_in, -1)
    return out


def _post_kernel(om_ref, of_ref, x_ref, mod_ref, onm_ref, onf_ref, wo_ref, nf_ref, wrh_ref, wrl_ref, br_ref,
                 x1_ref, hg_ref):
    o = jnp.concatenate([_rms(om_ref[...], onm_ref[...]), _rms(of_ref[...], onf_ref[...])], axis=1)
    mix = _dot(o.astype(BF16), wo_ref[...])
    gate1 = mod_ref[0, 2:3, :]
    shift2 = mod_ref[0, 3:4, :]
    scale2 = mod_ref[0, 4:5, :]
    x1 = x_ref[...] + gate1 * mix
    x1_ref[...] = x1
    h2 = _rms(x1, nf_ref[...]) * (1.0 + scale2) + shift2
    hh = h2.astype(BF16)
    hg_ref[:, :h2.shape[1]] = hh.astype(F32)
    hl = (h2 - hh.astype(F32)).astype(BF16)
    wrh = wrh_ref[...]
    logits = _dot(hh, wrh) + _dot(hl, wrh) + _dot(hh, wrl_ref[...]) + br_ref[...]

    lane = lax.broadcasted_iota(jnp.int32, logits.shape, 1).astype(F32)
    big = float(LANES)
    gmask = lane < N_GROUPS
    gl = jnp.where(gmask, logits, NEG)
    gmax = jnp.max(gl, axis=1, keepdims=True)
    gsum = jnp.sum(jnp.where(gmask, jnp.exp(gl - gmax), 0.0), axis=1, keepdims=True)
    g_p = 1.0 / gsum
    g_idx = jnp.min(jnp.where(gl == gmax, lane, big), axis=1, keepdims=True)
    lo = ROUTER_OFF + g_idx * EXPERTS_PER_GROUP
    emask = (lane >= lo) & (lane < lo + EXPERTS_PER_GROUP)
    el = jnp.where(emask, logits, NEG)
    emax = jnp.max(el, axis=1, keepdims=True)
    ee = jnp.where(emask, jnp.exp(el - emax), 0.0)
    ep = ee / jnp.sum(ee, axis=1, keepdims=True)
    p1 = jnp.max(ep, axis=1, keepdims=True)
    i1 = jnp.min(jnp.where(emask & (ep == p1), lane, big), axis=1, keepdims=True)
    rest = emask & (lane != i1)
    ep2 = jnp.where(rest, ep, -1.0)
    p2 = jnp.max(ep2, axis=1, keepdims=True)
    i2 = jnp.min(jnp.where(rest & (ep2 == p2), lane, big), axis=1, keepdims=True)
    den = p1 + p2
    gate = (jnp.where(lane == i1, g_p * (p1 / den), 0.0) + jnp.where(lane == i2, g_p * (p2 / den), 0.0))
    g8 = jnp.zeros_like(gate)
    for g in range(N_GROUPS):
        g8 = g8 + pltpu.roll(gate, LANES - (ROUTER_OFF + g * EXPERTS_PER_GROUP), axis=1)
    d = h2.shape[1]
    hg_ref[:, d:] = jnp.where(lane < EXPERTS_PER_GROUP, g8, jnp.where(lane == GIDX_LANE, g_idx, 0.0))


def _post_attention(om, of, x, mod8, lw, tm):
    B, T, D = x.shape
    nt = T // tm
    N = B * T
    row = lambda w: pl.BlockSpec((tm, w), lambda b, i: (b * nt + i, 0))
    full = lambda a: pl.BlockSpec(a.shape, lambda b, i: (0,) * a.ndim)
    weights = [lw["onm"], lw["onf"], lw["wo"], lw["nf"], lw["wrh"], lw["wrl"], lw["br"]]
    HV = H_MLA * V_DIM
    return pl.pallas_call(
        _post_kernel,
        out_shape=[jax.ShapeDtypeStruct((N, D), F32), jax.ShapeDtypeStruct((N, D + LANES), F32)],
        grid=(B, nt),
        in_specs=[row(HV), row(HV), row(D), pl.BlockSpec((1, 8, D), lambda b, i: (b, 0, 0))]
                 + [full(w) for w in weights],
        out_specs=[row(D), row(D + LANES)],
        compiler_params=_cparams(("arbitrary", "arbitrary")),
        name="post_attention",
    )(om, of, x.reshape(N, D), mod8, *weights)


def _row_copies(n, make, start):
    def body(r, c):
        cp = make(r)
        cp.start() if start else cp.wait()
        return c
    lax.fori_loop(0, n, body, 0, unroll=8)


def _dispatch_kernel(pos_ref, h_ref, hs_in, hs_out, sem, *, tm):
    del hs_in

    def copy(r, start):
        dst = pos_ref[0, 0, r] if start else 0
        return pltpu.make_async_copy(h_ref.at[pl.ds(r, 1), :], hs_out.at[pl.ds(dst, 1), :], sem.at[0])

    _row_copies(tm, lambda r: copy(r, True), True)
    _row_copies(tm, lambda r: copy(r, False), False)


def _dispatch(hg, pos3, R, tm):
    N, W = hg.shape
    return pl.pallas_call(
        functools.partial(_dispatch_kernel, tm=tm),
        out_shape=jax.ShapeDtypeStruct((R, W), F32),
        grid=(N // tm,),
        in_specs=[pl.BlockSpec((1, 1, tm), lambda t: (t, 0, 0), memory_space=pltpu.SMEM),
                  pl.BlockSpec((tm, W), lambda t: (t, 0)), pl.BlockSpec(memory_space=pl.ANY)],
        out_specs=pl.BlockSpec(memory_space=pl.ANY),
        scratch_shapes=[pltpu.SemaphoreType.DMA((1,))],
        input_output_aliases={2: 0},
        compiler_params=_cparams(("arbitrary",)),
        name="moe_dispatch",
    )(pos3, hg, jnp.zeros((R, W), F32))


def _moe_kernel(tg_ref, hs_ref, wg_ref, wu_ref, wd_ref, ys_ref, *, d):
    del tg_ref
    rows = hs_ref[...]
    h = rows[:, :d].astype(BF16)
    g8 = rows[:, d:]
    lane = lax.broadcasted_iota(jnp.int32, g8.shape, 1)
    acc = jnp.zeros((rows.shape[0], d), F32)
    for j in range(EXPERTS_PER_GROUP):
        a = _dot(h, wg_ref[j])
        u = _dot(h, wu_ref[j])
        gcol = jnp.sum(jnp.where(lane == j, g8, 0.0), axis=1, keepdims=True)
        act = (a * (1.0 / (1.0 + jnp.exp(-a)))) * u * gcol
        acc = acc + _dot(act.astype(BF16), wd_ref[j])
    ys_ref[...] = acc


def _moe(hg, lw, tm, tmd):
    N, W = hg.shape
    d = W - LANES
    R = N + N_GROUPS * tm
    nt = R // tm
    grp = hg[:, d + GIDX_LANE].astype(jnp.int32)
    onehot = (grp[:, None] == jnp.arange(N_GROUPS, dtype=jnp.int32)[None, :]).astype(jnp.int32)
    counts = jnp.sum(onehot, axis=0)
    padded = ((counts + tm - 1) // tm) * tm
    ends = jnp.cumsum(padded)
    starts = ends - padded
    rank = jnp.sum((jnp.cumsum(onehot, axis=0) - 1) * onehot, axis=1)
    pos = jnp.sum(onehot * starts[None, :], axis=1) + rank
    tile_first = jnp.arange(nt, dtype=jnp.int32) * tm
    tile_group = jnp.minimum(jnp.sum((tile_first[:, None] >= ends[None, :]).astype(jnp.int32), axis=1),
                             N_GROUPS - 1)

    hs = _dispatch(hg, pos.reshape(N // tmd, 1, tmd), R, tmd)
    wspec = lambda a: pl.BlockSpec((EXPERTS_PER_GROUP,) + a.shape[1:], lambda t, tg: (tg[t], 0, 0))
    grid_spec = pltpu.PrefetchScalarGridSpec(
        num_scalar_prefetch=1,
        grid=(nt,),
        in_specs=[pl.BlockSpec((tm, W), lambda t, tg: (t, 0)),
                  wspec(lw["wg"]), wspec(lw["wu"]), wspec(lw["wd"])],
        out_specs=pl.BlockSpec((tm, d), lambda t, tg: (t, 0)),
    )
    ys = pl.pallas_call(
        functools.partial(_moe_kernel, d=d),
        out_shape=jax.ShapeDtypeStruct((R, d), F32),
        grid_spec=grid_spec,
        compiler_params=_cparams(("arbitrary",)),
        name="moe_experts",
    )(tile_group, hs, lw["wg"], lw["wu"], lw["wd"])
    return ys, pos


def _residual_kernel(pos_ref, nxt_ref, x_ref, mod_ref, ys_hbm, o_ref, ybuf, sem, *, tm):
    nt = pl.num_programs(1)
    step = pl.program_id(0) * nt + pl.program_id(1)
    total = pl.num_programs(0) * nt
    slot = step % 2

    def copy(idx_ref, r, s, start):
        src = idx_ref[0, 0, r] if start else 0
        return pltpu.make_async_copy(ys_hbm.at[pl.ds(src, 1), :], ybuf.at[s, pl.ds(r, 1), :], sem.at[s])

    @pl.when(step == 0)
    def _():
        _row_copies(tm, lambda r: copy(pos_ref, r, 0, True), True)

    @pl.when(step + 1 < total)
    def _():
        _row_copies(tm, lambda r: copy(nxt_ref, r, 1 - slot, True), True)

    _row_copies(tm, lambda r: copy(pos_ref, r, slot, False), False)
    o_ref[...] = x_ref[...] + mod_ref[0, 5:6, :] * ybuf[slot]


def _residual(x1, ys, pos, mod8, tm):
    B = mod8.shape[0]
    N, D = x1.shape
    nt = N // B // tm
    ns = N // tm
    pos3 = pos.reshape(ns, 1, tm)
    row = pl.BlockSpec((tm, D), lambda b, i: (b * nt + i, 0))
    idx = lambda f: pl.BlockSpec((1, 1, tm), f, memory_space=pltpu.SMEM)
    return pl.pallas_call(
        functools.partial(_residual_kernel, tm=tm),
        out_shape=jax.ShapeDtypeStruct((N, D), F32),
        grid=(B, nt),
        in_specs=[idx(lambda b, i: (b * nt + i, 0, 0)),
                  idx(lambda b, i: (jnp.minimum(b * nt + i + 1, ns - 1), 0, 0)),
                  row, pl.BlockSpec((1, 8, D), lambda b, i: (b, 0, 0)), pl.BlockSpec(memory_space=pl.ANY)],
        out_specs=row,
        scratch_shapes=[pltpu.VMEM((2, tm, D), F32), pltpu.SemaphoreType.DMA((2,))],
        compiler_params=_cparams(("arbitrary", "arbitrary")),
        name="ffn_residual",
    )(pos3, pos3, x1, mod8, ys)


def _norm_kernel(x_ref, g_ref, o_ref):
    o_ref[...] = _rms(x_ref[...], g_ref[...])


def _final_norm(x, g, tm):
    B, T, D = x.shape
    N = B * T
    return pl.pallas_call(
        _norm_kernel,
        out_shape=jax.ShapeDtypeStruct((N, D), F32),
        grid=(N // tm,),
        in_specs=[pl.BlockSpec((tm, D), lambda i: (i, 0)), pl.BlockSpec((1, D), lambda i: (0, 0))],
        out_specs=pl.BlockSpec((tm, D), lambda i: (i, 0)),
        compiler_params=_cparams(("arbitrary",)),
        name="final_norm",
    )(x.reshape(N, D), g.reshape(1, D)).reshape(B, T, D)


def _rope_tables(p0, T):
    half = ROPE_DIM // 2
    inv = ROPE_THETA ** (-jnp.arange(half, dtype=F32) / half)
    ang = (p0 + jnp.arange(T, dtype=jnp.int32)).astype(F32)[:, None] * inv[None, :]
    cos, sin = jnp.cos(ang), jnp.sin(ang)
    z = lambda w: jnp.zeros((T, w), F32)
    one = jnp.ones((T, NOPE_DIM), F32)
    tail = z(HEAD_PAD - NOPE_DIM - ROPE_DIM)
    qs = MLA_SCALE * LOG2E
    cq = jnp.concatenate([one, cos, cos, tail], axis=1) * qs
    s1q = jnp.concatenate([z(NOPE_DIM), -sin, z(half), tail], axis=1) * qs
    s2q = jnp.concatenate([z(NOPE_DIM), z(half), sin, tail], axis=1) * qs
    kt = z(LANES - ROPE_DIM)
    ck = jnp.concatenate([cos, cos, kt], axis=1)
    s1k = jnp.concatenate([-sin, z(half), kt], axis=1)
    s2k = jnp.concatenate([z(half), sin, kt], axis=1)
    return cq, s1q, s2q, ck, s1k, s2k


def _layer_weights(l, p):
    D = p["w_in"].shape[1]
    w_in = p["w_in"][l]
    o = np.cumsum([0, Q_LORA, KV_LORA, ROPE_DIM, H_FOX * FOX_DIM, H_FOX * FOX_DIM, H_FOX * FOX_DIM, H_FOX])
    seg = lambda i: w_in[:, o[i]:o[i + 1]]
    padl = lambda w: jnp.pad(w, ((0, 0), (0, LANES - w.shape[1])))
    r1 = lambda v: v.reshape(1, -1).astype(F32)
    wqu = p["w_q_up"][l].reshape(Q_LORA, H_MLA, NOPE_DIM + ROPE_DIM)
    wqu = jnp.pad(wqu, ((0, 0), (0, 0), (0, HEAD_PAD - NOPE_DIM - ROPE_DIM))).reshape(Q_LORA, H_MLA * HEAD_PAD)
    wkv = p["w_kv_up"][l].reshape(KV_LORA, H_MLA, NOPE_DIM + V_DIM)
    wk_c = jnp.pad(wkv[:, :, :NOPE_DIM], ((0, 0), (0, 0), (0, HEAD_PAD - NOPE_DIM)))
    place = np.zeros((LANES, H_MLA, HEAD_PAD), np.float32)
    for j in range(ROPE_DIM):
        place[j, :, NOPE_DIM + j] = 1.0
    wk = jnp.concatenate([wk_c.reshape(KV_LORA, H_MLA * HEAD_PAD),
                          jnp.asarray(place.reshape(LANES, H_MLA * HEAD_PAD))], axis=0)
    wv = wkv[:, :, NOPE_DIM:].reshape(KV_LORA, H_MLA * V_DIM)
    wr = padl(jnp.concatenate([p["w_router_group"][l], p["w_router_expert"][l]], axis=1))
    wrh = wr.astype(BF16)
    wrl = (wr - wrh.astype(F32)).astype(BF16)
    br = padl(jnp.concatenate([p["b_router_group"][l], p["b_router_expert"][l]]).reshape(1, -1)).astype(F32)
    return dict(
        na=r1(p["norm_attn"][l]), wq=seg(0).astype(BF16), wkv=seg(1).astype(BF16), wkr=padl(seg(2)).astype(BF16),
        wfq=seg(3).astype(BF16), wfk=seg(4).astype(BF16), wfv=seg(5).astype(BF16), wfl=padl(seg(6)).astype(BF16),
        bfl=padl(r1(p["b_forget"][l])), qn=r1(p["q_norm"][l]), wqu=wqu.astype(BF16), kvn=r1(p["kv_norm"][l]),
        wk=wk.astype(BF16), wvt=wv.T.astype(BF16), wfvt=seg(5).T.astype(BF16),
        onm=r1(p["out_norm_mla"][l]), onf=r1(p["out_norm_fox"][l]), wo=p["w_out"][l].astype(BF16),
        nf=r1(p["norm_ffn"][l]), wrh=wrh, wrl=wrl, br=br,
        wg=p["w_gate"][l].astype(BF16), wu=p["w_up"][l].astype(BF16), wd=p["w_down"][l].astype(BF16),
    )


def _tile(n, pref):
    return pref if n % pref == 0 else n


def _layer(x, mod8, lw, past, cfg):
    B, T, D = x.shape
    tm = cfg["tm"]
    aug_consts = (cfg["tril3"], cfg["pq"], cfg["pk"])
    if past is None:
        P = 0
        cinit = jnp.zeros((B, 8, LANES), F32)
    else:
        P = past[0].shape[1]
        augk_past, cinit = _past_forget(past[4].astype(F32), (cfg["tril3_past"], cfg["pq"], cfg["pk"]),
                                        cfg["tm_past"])
    (ckv, kro, krp, kf, vf, lf, qm, qf, kfb, vfb, aq, ak, vft) = _pre_attention(
        x, mod8, lw, cfg["tabs"], aug_consts, cinit, tm)
    new_rows = (ckv.reshape(B, T, KV_LORA), kro.reshape(B, T, ROPE_DIM),
                kf.reshape(B, T, H_FOX, FOX_DIM), vf.reshape(B, T, H_FOX, FOX_DIM), lf.reshape(B, T, H_FOX))

    S = P + T
    if past is None:
        ckv_all, krp_all, kfb_all, ak_all = ckv, krp, kfb, ak
        Sp = S
    else:
        Sp = -(-S // cfg["tk"]) * cfg["tk"]
        HD = H_FOX * FOX_DIM

        def cat(old, new, w):
            a = jnp.concatenate([old.reshape(B, P, w).astype(new.dtype), new.reshape(B, T, w)], axis=1)
            return jnp.pad(a, ((0, 0), (0, Sp - S), (0, 0))).reshape(B * Sp, w)

        ckv_all = cat(past[0], ckv, KV_LORA)
        krp_all = cat(jnp.pad(past[1], ((0, 0), (0, 0), (0, LANES - ROPE_DIM))), krp, LANES)
        kfb_all = cat(past[2], kfb, HD)
        vft = _vt_from_rows(cat(past[3], vfb, HD), B, Sp)
        ak_all = cat(augk_past, ak, HD)
    km, vmt = _kv_up(ckv_all, krp_all, lw["wk"], lw["wvt"], B, Sp, _tile(Sp, 512))
    att = dict(B=B, T=T, S=Sp, tq=cfg["tq"], tk=cfg["tk"], p0=P)
    om = _attention((qm,), (km,), vmt, fox=False, **att)
    of = _attention((qf, aq), (kfb_all, ak_all), vft, fox=True, **att)
    x1, hg = _post_attention(om, of, x, mod8, lw, cfg["tm_post"])
    ys, pos = _moe(hg, lw, cfg["tms"], cfg["tms"])
    x2 = _residual(x1, ys, pos, mod8, tm).reshape(B, T, D)
    return x2, new_rows


def _group_cfg(B, T, P, pq, pk):
    tm = _tile(T, 256)
    tq = _tile(T, 512)
    tk = tq if P == 0 else (512 if P % 512 == 0 else LANES)
    cfg = dict(tm=tm, tm_post=_tile(T, 512), tq=tq, tk=tk, tms=_tile(B * T, 512), tril3=_tril3(tm), pq=pq, pk=pk,
               tabs=_rope_tables(P, T))
    if P:
        cfg["tm_past"] = _tile(P, 256)
        cfg["tril3_past"] = _tril3(cfg["tm_past"])
    return cfg


def kernel(x_prompt, x_sample, cache_mla_ckv, cache_mla_krope, cache_fox_k, cache_fox_v, cache_fox_logf, c_prompt, c_sample, w_mod, b_mod, norm_attn, w_in, b_forget, q_norm, w_q_up, kv_norm, w_kv_up, out_norm_mla, out_norm_fox, w_out, norm_ffn, w_router_group, b_router_group, w_router_expert, b_router_expert, w_gate, w_up, w_down, final_norm):
    params = dict(norm_attn=norm_attn, w_in=w_in, b_forget=b_forget, q_norm=q_norm, w_q_up=w_q_up,
                  kv_norm=kv_norm, w_kv_up=w_kv_up, out_norm_mla=out_norm_mla, out_norm_fox=out_norm_fox,
                  w_out=w_out, norm_ffn=norm_ffn, w_router_group=w_router_group, b_router_group=b_router_group,
                  w_router_expert=w_router_expert, b_router_expert=b_router_expert,
                  w_gate=w_gate, w_up=w_up, w_down=w_down)
    L = w_mod.shape[0]
    Bp, Tp, D = x_prompt.shape
    Bs, Ts, _ = x_sample.shape
    P = cache_mla_ckv.shape[2]

    R = -(-(Bp + Bs) // 8) * 8
    c_all = jnp.pad(jnp.concatenate([c_prompt, c_sample], axis=0), ((0, R - Bp - Bs), (0, 0)))
    mod = _mod_all(c_all, w_mod, b_mod).reshape(L, R, 6, D)
    mod = jnp.pad(mod, ((0, 0), (0, 0), (0, 2), (0, 0)))

    pq, pk = _aug_placement()
    cfg_p = _group_cfg(Bp, Tp, 0, pq, pk)
    cfg_s = _group_cfg(Bs, Ts, P, pq, pk)

    y_p, y_s = x_prompt, x_sample
    rows_p, rows_s = [], []
    for l in range(L):
        lw = _layer_weights(l, params)
        y_p, r_p = _layer(y_p, mod[l, :Bp], lw, None, cfg_p)
        past = (cache_mla_ckv[l], cache_mla_krope[l], cache_fox_k[l], cache_fox_v[l], cache_fox_logf[l])
        y_s, r_s = _layer(y_s, mod[l, Bp:Bp + Bs], lw, past, cfg_s)
        rows_p.append(r_p)
        rows_s.append(r_s)

    y_prompt = _final_norm(y_p, final_norm, _tile(Bp * Tp, 512))
    y_sample = _final_norm(y_s, final_norm, _tile(Bs * Ts, 512))
    stack = lambda rows, k: jnp.stack([r[k] for r in rows])
    return (y_prompt, y_sample,
            stack(rows_p, 0), stack(rows_p, 1), stack(rows_p, 2), stack(rows_p, 3), stack(rows_p, 4),
            stack(rows_s, 0), stack(rows_s, 1), stack(rows_s, 2), stack(rows_s, 3), stack(rows_s, 4))
```
